```python
import jax
import jax.numpy as jnp
from jax import lax
import numpy as np

D_MODEL = 1024
BATCH = 16
SEQ = 4096
DEPTH = 1

SB_HEADS = 16
SB_HEAD_DIM = 64
SB_WIDTH = SB_HEADS * SB_HEAD_DIM
SB_QUERY_BLOCK = 128
SSD_EXPAND = 2
SSD_D_INNER = SSD_EXPAND * D_MODEL
SSD_HEAD_DIM = 64
SSD_HEADS = SSD_D_INNER // SSD_HEAD_DIM
SSD_GROUPS = 8
SSD_HEADS_PER_GROUP = SSD_HEADS // SSD_GROUPS
SSD_D_STATE = 128
SSD_CONV = 4
SSD_CHUNK = 128
SSD_CONV_DIM = SSD_D_INNER + 2 * SSD_GROUPS * SSD_D_STATE
PROJ_SPLITS = (SB_WIDTH, SB_WIDTH, SB_WIDTH, SSD_D_INNER, SSD_CONV_DIM, SSD_HEADS, D_MODEL, D_MODEL)
PROJ_DIM = sum(PROJ_SPLITS)
MOE_GROUPS = 4
MOE_EXPERTS_PER_GROUP = 8
MOE_EXPERTS = MOE_GROUPS * MOE_EXPERTS_PER_GROUP
MOE_TOP_K = 2
MOE_D_FF = 512
DEEPNORM_ALPHA = (2 * DEPTH) ** 0.25
DEEPNORM_BETA = (8 * DEPTH) ** -0.25
LN_EPS = 1e-5
RMS_EPS = 1e-5

kernel_name = "hybrid_sb_ssd_hmoe_deepnorm_adaln"


def layer_norm(x, g, b):
    xf = x.astype(jnp.float32)
    mu = jnp.mean(xf, axis=-1, keepdims=True)
    xc = xf - mu
    var = jnp.mean(xc * xc, axis=-1, keepdims=True)
    return (xc * lax.rsqrt(var + LN_EPS) * g + b).astype(x.dtype)


def stick_breaking_attention(q, k, v):
    bsz, seqlen, nh, hd = q.shape
    scale = hd ** -0.5
    outs = []
    for blk in range(seqlen // SB_QUERY_BLOCK):
        start = blk * SB_QUERY_BLOCK
        end = start + SB_QUERY_BLOCK
        logits = jnp.einsum("bqhd,bkhd->bhqk", q[:, start:end], k[:, :end]).astype(jnp.float32) * scale
        t_pos = start + jnp.arange(SB_QUERY_BLOCK)
        s_pos = jnp.arange(end)
        strictly_past = s_pos[None, :] < t_pos[:, None]
        log_not = jnp.where(strictly_past, jax.nn.log_sigmoid(-logits), 0.0)
        log_between = lax.cumsum(log_not, axis=3, reverse=True) - log_not
        weights = jnp.where(strictly_past, jnp.exp(jax.nn.log_sigmoid(logits) + log_between), 0.0)
        outs.append(jnp.einsum("bhqk,bkhd->bqhd", weights.astype(v.dtype), v[:, :end]))
    return jnp.concatenate(outs, axis=1)


def causal_depthwise_conv(u, w, b):
    ksz = w.shape[0]
    out = lax.conv_general_dilated(
        u, w[:, None, :], window_strides=(1,), padding=[(ksz - 1, 0)],
        dimension_numbers=("NWC", "WIO", "NWC"), feature_group_count=u.shape[-1])
    return out + b


def segsum_from_cumsum(cs):
    t = cs.shape[-1]
    mask = jnp.tril(jnp.ones((t, t), dtype=bool))
    return jnp.where(mask, cs[..., :, None] - cs[..., None, :], -jnp.inf)


def mamba2_ssd(z, xbc, dt, conv_w, conv_b, dt_bias, a_log, d_skip, norm_w):
    bsz, seqlen, _ = z.shape
    nc = seqlen // SSD_CHUNK
    gs, rs, ps, ns = SSD_GROUPS, SSD_HEADS_PER_GROUP, SSD_HEAD_DIM, SSD_D_STATE
    xbc = jax.nn.silu(causal_depthwise_conv(xbc, conv_w, conv_b))
    xs, bm, cm = jnp.split(xbc, [SSD_D_INNER, SSD_D_INNER + gs * ns], axis=-1)
    dt = jax.nn.softplus((dt + dt_bias).astype(jnp.float32))
    a = -jnp.exp(a_log.astype(jnp.float32))
    xh = xs.reshape(bsz, seqlen, SSD_HEADS, ps)
    x_dt = (xh * dt[..., None]).reshape(bsz, nc, SSD_CHUNK, gs, rs, ps)
    bm = bm.reshape(bsz, nc, SSD_CHUNK, gs, ns)
    cm = cm.reshape(bsz, nc, SSD_CHUNK, gs, ns)
    a_dt = (a * dt).reshape(bsz, nc, SSD_CHUNK, gs, rs).transpose(0, 3, 4, 1, 2)
    a_cs = jnp.cumsum(a_dt, axis=-1)
    decay_in = jnp.exp(segsum_from_cumsum(a_cs))
    cb = jnp.einsum("bclgn,bcsgn->bgcls", cm, bm)
    y_diag = jnp.einsum("bgcls,bgrcls,bcsgrp->bclgrp", cb, decay_in, x_dt)
    decay_to_end = jnp.exp(a_cs[..., -1:] - a_cs)
    states = jnp.einsum("bclgn,bgrcl,bclgrp->bcgrpn", bm, decay_to_end, x_dt)
    chunk_tot = jnp.pad(a_cs[..., -1], ((0, 0), (0, 0), (0, 0), (1, 0)))
    decay_chunk = jnp.exp(segsum_from_cumsum(jnp.cumsum(chunk_tot, axis=-1)))
    states = jnp.concatenate([jnp.zeros_like(states[:, :1]), states], axis=1)
    states_in = jnp.einsum("bgrzc,bcgrpn->bzgrpn", decay_chunk, states)[:, :-1]
    y_off = jnp.einsum("bclgn,bcgrpn,bgrcl->bclgrp", cm, states_in, jnp.exp(a_cs))
    y = (y_diag + y_off).reshape(bsz, seqlen, SSD_HEADS, ps) + d_skip[:, None] * xh
    y = y.reshape(bsz, seqlen, SSD_D_INNER)
    yg = (y * jax.nn.silu(z)).astype(jnp.float32).reshape(bsz, seqlen, gs, SSD_D_INNER // gs)
    yg = yg * lax.rsqrt(jnp.mean(yg * yg, axis=-1, keepdims=True) + RMS_EPS)
    return (yg.reshape(bsz, seqlen, SSD_D_INNER) * norm_w).astype(z.dtype)


def token_mixer(h, w_in, conv_w, conv_b, dt_bias, a_log, d_skip, ssd_norm_w, w_sb_out, w_ssd_out, w_out):
    bsz, seqlen, _ = h.shape
    split_points = np.cumsum(PROJ_SPLITS)[:-1].tolist()
    proj = jnp.einsum("bsd,de->bse", h, w_in)
    q, k, v, z, xbc, dt, g_sb, g_ssd = jnp.split(proj, split_points, axis=-1)
    q = q.reshape(bsz, seqlen, SB_HEADS, SB_HEAD_DIM)
    k = k.reshape(bsz, seqlen, SB_HEADS, SB_HEAD_DIM)
    v = v.reshape(bsz, seqlen, SB_HEADS, SB_HEAD_DIM)
    y_sb = stick_breaking_attention(q, k, v).reshape(bsz, seqlen, SB_WIDTH)
    y_ssd = mamba2_ssd(z, xbc, dt, conv_w, conv_b, dt_bias, a_log, d_skip, ssd_norm_w)
    a_sb = jnp.einsum("bsc,cd->bsd", y_sb, w_sb_out)
    a_ssd = jnp.einsum("bsc,cd->bsd", y_ssd, w_ssd_out)
    merged = jax.nn.sigmoid(g_sb) * a_sb + jax.nn.sigmoid(g_ssd) * a_ssd
    return jnp.einsum("bsd,de->bse", merged, w_out)


def hierarchical_moe(h, w_group, b_group, w_router, b_router, w_gate_e, w_up_e, w_down_e):
    bsz, seqlen, d = h.shape
    tok = h.reshape(-1, d)
    group_probs = jax.nn.softmax((tok @ w_group + b_group).astype(jnp.float32), axis=-1)
    g_prob, g_idx = lax.top_k(group_probs, 1)
    expert_logits = (tok @ w_router + b_router).astype(jnp.float32).reshape(-1, MOE_GROUPS, MOE_EXPERTS_PER_GROUP)
    in_group = jnp.take_along_axis(expert_logits, g_idx[:, :, None], axis=1)[:, 0]
    top_logit, top_idx = lax.top_k(in_group, MOE_TOP_K)
    top_w = jax.nn.softmax(top_logit, axis=-1) * g_prob
    expert_id = g_idx * MOE_EXPERTS_PER_GROUP + top_idx
    combine = jnp.sum(jax.nn.one_hot(expert_id, MOE_EXPERTS, dtype=jnp.float32) * top_w[..., None], axis=1)
    combine = combine.astype(tok.dtype)
    out = jnp.zeros(tok.shape, tok.dtype)
    for e in range(MOE_EXPERTS):
        hid = jax.nn.silu(tok @ w_gate_e[e]) * (tok @ w_up_e[e])
        out = out + combine[:, e:e + 1] * (hid @ w_down_e[e])
    return out.reshape(bsz, seqlen, d)


def setup_inputs(seed: int = 0) -> dict:
    key = jax.random.key(seed)
    ks = jax.random.split(key, 26)
    f32 = jnp.float32
    nrm = lambda k, shape, s: jax.random.normal(k, shape, f32) * s
    dt0 = jnp.exp(jax.random.uniform(ks[5], (DEPTH, SSD_HEADS), f32, np.log(1e-3), np.log(1e-1)))
    return {
        "x": nrm(ks[0], (BATCH, SEQ, D_MODEL), 1.0),
        "c": nrm(ks[1], (BATCH, D_MODEL), 1.0),
        "w_ada": nrm(ks[2], (DEPTH, D_MODEL, 6 * D_MODEL), D_MODEL ** -0.5),
        "b_ada": nrm(ks[3], (DEPTH, 6 * D_MODEL), 0.02),
        "w_in": nrm(ks[4], (DEPTH, D_MODEL, PROJ_DIM), D_MODEL ** -0.5),
        "conv_w": nrm(ks[6], (DEPTH, SSD_CONV, SSD_CONV_DIM), SSD_CONV ** -0.5),
        "conv_b": nrm(ks[7], (DEPTH, SSD_CONV_DIM), 0.02),
        "dt_bias": dt0 + jnp.log(-jnp.expm1(-dt0)),
        "a_log": jnp.log(jax.random.uniform(ks[8], (DEPTH, SSD_HEADS), f32, 1.0, 16.0)),
        "d_skip": 1.0 + nrm(ks[9], (DEPTH, SSD_HEADS), 0.02),
        "ssd_norm_w": 1.0 + nrm(ks[10], (DEPTH, SSD_D_INNER), 0.02),
        "w_sb_out": nrm(ks[11], (DEPTH, SB_WIDTH, D_MODEL), SB_WIDTH ** -0.5 * DEEPNORM_BETA),
        "w_ssd_out": nrm(ks[12], (DEPTH, SSD_D_INNER, D_MODEL), SSD_D_INNER ** -0.5 * DEEPNORM_BETA),
        "w_out": nrm(ks[13], (DEPTH, D_MODEL, D_MODEL), D_MODEL ** -0.5 * DEEPNORM_BETA),
        "ln1_g": 1.0 + nrm(ks[14], (DEPTH, D_MODEL), 0.02),
        "ln1_b": nrm(ks[15], (DEPTH, D_MODEL), 0.02),
        "w_group": nrm(ks[16], (DEPTH, D_MODEL, MOE_GROUPS), D_MODEL ** -0.5),
        "b_group": nrm(ks[17], (DEPTH, MOE_GROUPS), 0.01),
        "w_router": nrm(ks[18], (DEPTH, D_MODEL, MOE_EXPERTS), D_MODEL ** -0.5),
        "b_router": nrm(ks[19], (DEPTH, MOE_EXPERTS), 0.01),
        "w_gate_e": nrm(ks[20], (DEPTH, MOE_EXPERTS, D_MODEL, MOE_D_FF), D_MODEL ** -0.5),
        "w_up_e": nrm(ks[21], (DEPTH, MOE_EXPERTS, D_MODEL, MOE_D_FF), D_MODEL ** -0.5),
        "w_down_e": nrm(ks[22], (DEPTH, MOE_EXPERTS, MOE_D_FF, D_MODEL), MOE_D_FF ** -0.5 * DEEPNORM_BETA),
        "ln2_g": 1.0 + nrm(ks[23], (DEPTH, D_MODEL), 0.02),
        "ln2_b": nrm(ks[24], (DEPTH, D_MODEL), 0.02),
    }


def reference(x, c, w_ada, b_ada, w_in, conv_w, conv_b, dt_bias, a_log, d_skip, ssd_norm_w,
              w_sb_out, w_ssd_out, w_out, ln1_g, ln1_b, w_group, b_group, w_router, b_router,
              w_gate_e, w_up_e, w_down_e, ln2_g, ln2_b):
    for l in range(DEPTH):
        mod = jnp.einsum("bd,de->be", jax.nn.silu(c), w_ada[l]) + b_ada[l]
        shift1, scale1, gate1, shift2, scale2, gate2 = jnp.split(mod[:, None, :], 6, axis=-1)
        h = x * (1.0 + scale1) + shift1
        y = token_mixer(h, w_in[l], conv_w[l], conv_b[l], dt_bias[l], a_log[l], d_skip[l],
                        ssd_norm_w[l], w_sb_out[l], w_ssd_out[l], w_out[l])
        x = layer_norm(DEEPNORM_ALPHA * x + gate1 * y, ln1_g[l], ln1_b[l])
        h = x * (1.0 + scale2) + shift2
        y = hierarchical_moe(h, w_group[l], b_group[l], w_router[l], b_router[l],
                             w_gate_e[l], w_up_e[l], w_down_e[l])
        x = layer_norm(DEEPNORM_ALPHA * x + gate2 * y, ln2_g[l], ln2_b[l])
    return x
```

```python
import functools

import jax
import jax.numpy as jnp
import numpy as np
from jax import lax
from jax.experimental import pallas as pl
from jax.experimental.pallas import tpu as pltpu

F32 = jnp.float32
BF16 = jnp.bfloat16

D_MODEL = 1024
SB_HEADS = 16
SB_HEAD_DIM = 64
SB_WIDTH = SB_HEADS * SB_HEAD_DIM
SSD_D_INNER = 2 * D_MODEL
SSD_HEAD_DIM = 64
SSD_HEADS = SSD_D_INNER // SSD_HEAD_DIM
SSD_GROUPS = 8
SSD_HEADS_PER_GROUP = SSD_HEADS // SSD_GROUPS
SSD_D_STATE = 128
SSD_CONV = 4
SSD_CHUNK = 128
SSD_GROUP_WIDTH = SSD_D_INNER // SSD_GROUPS
SSD_CONV_DIM = SSD_D_INNER + 2 * SSD_GROUPS * SSD_D_STATE
MOE_GROUPS = 4
MOE_EXPERTS_PER_GROUP = 8
MOE_EXPERTS = MOE_GROUPS * MOE_EXPERTS_PER_GROUP
MOE_D_FF = 512
DEPTH = 1
DEEPNORM_ALPHA = (2 * DEPTH) ** 0.25
LN_EPS = 1e-5
RMS_EPS = 1e-5

LANES = 128
SUBLANES = 8

COL_XBC = 0
COL_Z = COL_XBC + SSD_CONV_DIM
COL_G = COL_Z + SSD_D_INNER
COL_Q = COL_G + 2 * D_MODEL
COL_K = COL_Q + SB_WIDTH
COL_V = COL_K + SB_WIDTH
PROJ_COLS = COL_V + SB_WIDTH

IN_TM = 1024
IN_TN = 1024
SB_TQ = 256
MIX_TM = 512
MOE_TM = 512
SB_LOG_WEIGHT_FLOOR = -110.0

ROUTE_E0, ROUTE_E1, ROUTE_R0, ROUTE_R1, ROUTE_W0, ROUTE_W1 = range(6)


def _split3(a):
    a1 = a.astype(BF16)
    r1 = a - a1.astype(F32)
    a2 = r1.astype(BF16)
    a3 = (r1 - a2.astype(F32)).astype(BF16)
    return a1, a2, a3


def _dot(a, b):
    return jnp.dot(a, b, preferred_element_type=F32)


def _dot_nt(a, b):
    return lax.dot_general(a, b, (((1,), (1,)), ((), ())), preferred_element_type=F32)


def _sigmoid(x):
    return 1.0 / (1.0 + jnp.exp(-x))


def _silu(x):
    return x * _sigmoid(x)


def _softplus(x):
    return jnp.maximum(x, 0.0) + jnp.log(1.0 + jnp.exp(-jnp.abs(x)))


def _ada_kernel(c_ref, w_ref, b_ref, o_ref):
    c = c_ref[...]
    o_ref[...] = _dot(_silu(c), w_ref[...]) + b_ref[...]


def _ada_mod(c, w_ada, b_ada):
    bsz = c.shape[0]
    n = w_ada.shape[1]
    tn = D_MODEL
    return pl.pallas_call(
        _ada_kernel,
        grid=(n // tn,),
        in_specs=[
            pl.BlockSpec((bsz, D_MODEL), lambda j: (0, 0)),
            pl.BlockSpec((D_MODEL, tn), lambda j: (0, j)),
            pl.BlockSpec((1, tn), lambda j: (0, j)),
        ],
        out_specs=pl.BlockSpec((bsz, tn), lambda j: (0, j)),
        out_shape=jax.ShapeDtypeStruct((bsz, n), F32),
        name="ada_mod",
    )(c, w_ada, b_ada.reshape(1, n))


def _in_proj_kernel(x_ref, sc_ref, sh_ref, w_ref, wdt_hi_ref, wdt_lo_ref, o_ref, dt_ref, h_ref):
    j = pl.program_id(2)

    @pl.when(j == 0)
    def _():
        h = x_ref[0] * (1.0 + sc_ref[0]) + sh_ref[0]
        h1, h2, _ = _split3(h)
        h_ref[...] = h1
        dt_ref[0] = _dot(h1, wdt_hi_ref[...]) + _dot(h2, wdt_hi_ref[...]) + _dot(h1, wdt_lo_ref[...])

    o_ref[0] = _dot(h_ref[...], w_ref[...]).astype(BF16)


def _in_proj(x, scale1, shift1, w_cat, wdt_hi, wdt_lo):
    bsz, seq, _ = x.shape
    tm, tn = min(IN_TM, seq), IN_TN
    return pl.pallas_call(
        _in_proj_kernel,
        grid=(bsz, seq // tm, PROJ_COLS // tn),
        in_specs=[
            pl.BlockSpec((1, tm, D_MODEL), lambda b, i, j: (b, i, 0)),
            pl.BlockSpec((1, 1, D_MODEL), lambda b, i, j: (b, 0, 0)),
            pl.BlockSpec((1, 1, D_MODEL), lambda b, i, j: (b, 0, 0)),
            pl.BlockSpec((D_MODEL, tn), lambda b, i, j: (0, j)),
            pl.BlockSpec((D_MODEL, LANES), lambda b, i, j: (0, 0)),
            pl.BlockSpec((D_MODEL, LANES), lambda b, i, j: (0, 0)),
        ],
        out_specs=[
            pl.BlockSpec((1, tm, tn), lambda b, i, j: (b, i, j)),
            pl.BlockSpec((1, tm, LANES), lambda b, i, j: (b, i, 0)),
        ],
        out_shape=[
            jax.ShapeDtypeStruct((bsz, seq, PROJ_COLS), BF16),
            jax.ShapeDtypeStruct((bsz, seq, LANES), F32),
        ],
        scratch_shapes=[pltpu.VMEM((tm, D_MODEL), BF16)],
        compiler_params=pltpu.CompilerParams(
            dimension_semantics=("parallel", "parallel", "arbitrary"),
            vmem_limit_bytes=48 * 1024 * 1024),
        name="in_proj",
    )(x, scale1, shift1, w_cat, wdt_hi, wdt_lo)


def _sb_kernel(q_ref, k_ref, v_ref, o_ref, acc_ref, r_ref, *, tq, scale):
    qi = pl.program_id(2)
    tk = tq
    q = q_ref[0]
    lane = lax.broadcasted_iota(jnp.int32, (tq, LANES), 1)
    row = lax.broadcasted_iota(jnp.int32, (tq, tk), 0)
    col = lax.broadcasted_iota(jnp.int32, (tq, tk), 1)
    strictly_past = col < row
    later_key = (row > col).astype(BF16)

    def tile(head, qh, kt, diag):
        start = pl.multiple_of(kt * tk, tk)
        k_blk = k_ref[0, pl.ds(start, tk), :]
        v_blk = v_ref[0, pl.ds(start, tk), :]
        s = _dot_nt(qh, k_blk) * scale
        log_not = jnp.minimum(-s, 0.0) - jnp.log(1.0 + jnp.exp(-jnp.abs(s)))
        if diag:
            log_not = jnp.where(strictly_past, log_not, 0.0)
        hi = log_not.astype(BF16)
        lo = (log_not - hi.astype(F32)).astype(BF16)
        between = _dot(hi, later_key) + _dot(lo, later_key)
        r = r_ref[...]
        carry = jnp.concatenate([r] * (tk // LANES), axis=1)
        w = jnp.exp(s + log_not + between + carry)
        if diag:
            w = jnp.where(strictly_past, w, 0.0)
        contrib = _dot(w.astype(BF16), v_blk)
        r_new = r + jnp.sum(log_not, axis=1, keepdims=True)
        r_ref[...] = r_new
        if diag:
            acc_ref[head] = contrib
        else:
            acc_ref[head] += contrib
        return (jnp.max(r_new) > SB_LOG_WEIGHT_FLOOR).astype(jnp.int32)

    for head in range(2):
        in_head = (lane >= head * SB_HEAD_DIM) & (lane < (head + 1) * SB_HEAD_DIM)
        qh = jnp.where(in_head, q, jnp.zeros_like(q))
        r_ref[...] = jnp.zeros_like(r_ref)
        go0 = tile(head, qh, qi, True)

        def cond(carry):
            kt, go = carry
            return jnp.logical_and(kt >= 0, go > 0)

        def body(carry, head=head, qh=qh):
            kt, _ = carry
            return kt - 1, tile(head, qh, kt, False)

        lax.while_loop(cond, body, (qi - 1, go0))

    o_ref[0] = jnp.where(lane < SB_HEAD_DIM, acc_ref[0], acc_ref[1]).astype(BF16)


def _sb_attention(proj):
    bsz, seq, _ = proj.shape
    tq = min(SB_TQ, seq)
    pairs = SB_WIDTH // LANES
    kern = functools.partial(_sb_kernel, tq=tq, scale=SB_HEAD_DIM ** -0.5)
    return pl.pallas_call(
        kern,
        grid=(bsz, pairs, seq // tq),
        in_specs=[
            pl.BlockSpec((1, tq, LANES), lambda b, p, i: (b, i, COL_Q // LANES + p)),
            pl.BlockSpec((1, seq, LANES), lambda b, p, i: (b, 0, COL_K // LANES + p)),
            pl.BlockSpec((1, seq, LANES), lambda b, p, i: (b, 0, COL_V // LANES + p)),
        ],
        out_specs=pl.BlockSpec((1, tq, LANES), lambda b, p, i: (b, i, p)),
        out_shape=jax.ShapeDtypeStruct((bsz, seq, SB_WIDTH), BF16),
        scratch_shapes=[pltpu.VMEM((2, tq, LANES), F32), pltpu.VMEM((tq, LANES), F32)],
        compiler_params=pltpu.CompilerParams(
            dimension_semantics=("parallel", "parallel", "arbitrary")),
        name="sb_attn",
    )(proj, proj, proj)


def _ssd_kernel(xbc_ref, z_ref, dt_ref, cw_ref, cb_ref, dtb_ref, alog_ref, dskip_ref, nw_ref, e2_ref,
                o_ref, ubuf, state_ref):
    c = pl.program_id(1)
    L = SSD_CHUNK
    GW = SSD_GROUP_WIDTH
    NS = SSD_D_STATE
    halo = SUBLANES

    @pl.when(c == 0)
    def _():
        state_ref[...] = jnp.zeros_like(state_ref)
        ubuf[0:halo, :] = jnp.zeros((halo, SSD_CONV_DIM), F32)

    ubuf[halo:halo + L, :] = xbc_ref[0].astype(F32)

    def conv(c0, width):
        acc = cb_ref[:, c0:c0 + width]
        for kk in range(SSD_CONV):
            off = halo - (SSD_CONV - 1) + kk
            acc = acc + cw_ref[kk:kk + 1, c0:c0 + width] * ubuf[off:off + L, c0:c0 + width]
        return _silu(acc)

    lane = lax.broadcasted_iota(jnp.int32, (1, LANES), 1)
    head_lane = lane < SSD_HEADS
    dtv = _softplus(dt_ref[0] + dtb_ref[...])
    a = jnp.where(head_lane, -jnp.exp(alog_ref[...]), 0.0)
    adt = a * dtv
    li = lax.broadcasted_iota(jnp.int32, (L, L), 0)
    si = lax.broadcasted_iota(jnp.int32, (L, L), 1)
    causal = si <= li
    tri = causal.astype(BF16)
    a1, a2, a3 = _split3(adt)
    cs = _dot(tri, a1) + _dot(tri, a2) + _dot(tri, a3)
    cs_t = cs.T
    cs_end = cs[L - 1:L, :]
    decay_to_end = jnp.exp(cs_end - cs)
    decay_from_start = jnp.exp(cs)

    def expand(v, g):
        hi = v.astype(BF16)
        lo = (v - hi.astype(F32)).astype(BF16)
        return _dot(jnp.concatenate([hi, lo], axis=1), e2_ref[:, g * GW:(g + 1) * GW])

    lane_gw = lax.broadcasted_iota(jnp.int32, (L, GW), 1)

    for g in range(SSD_GROUPS):
        xs = conv(g * GW, GW)
        bmat = conv(SSD_D_INNER + g * NS, NS).astype(BF16)
        cmat = conv(SSD_D_INNER + SSD_GROUPS * NS + g * NS, NS).astype(BF16)
        cb = _dot_nt(cmat, bmat)
        dt_e = expand(dtv, g)
        xdt = xs * dt_e
        xdt_b = xdt.astype(BF16)
        xdte_b = (xdt * expand(decay_to_end, g)).astype(BF16)
        from_start_e = expand(decay_from_start, g)
        y = jnp.zeros((L, GW), F32)
        for r in range(SSD_HEADS_PER_GROUP):
            h = g * SSD_HEADS_PER_GROUP + r
            seg = jnp.broadcast_to(cs[:, h:h + 1], (L, L)) - cs_t[h:h + 1, :]
            decay = jnp.exp(jnp.where(causal, seg, -jnp.inf))
            m = (cb * decay).astype(BF16)
            in_head = (lane_gw >= r * SSD_HEAD_DIM) & (lane_gw < (r + 1) * SSD_HEAD_DIM)
            y = y + _dot(m, jnp.where(in_head, xdt_b, jnp.zeros_like(xdt_b)))
        state = state_ref[g]
        y = y + _dot(cmat, state.astype(BF16)) * from_start_e
        new_state = lax.dot_general(bmat, xdte_b, (((0,), (0,)), ((), ())), preferred_element_type=F32)
        state_ref[g] = state * from_start_e[L - 1:L, :] + new_state
        y = y + dskip_ref[:, g * GW:(g + 1) * GW] * xs
        zg = z_ref[0, :, g * GW:(g + 1) * GW].astype(F32)
        yg = y * _silu(zg)
        ms = jnp.mean(yg * yg, axis=1, keepdims=True)
        o_ref[0, :, g * GW:(g + 1) * GW] = (
            yg * lax.rsqrt(ms + RMS_EPS) * nw_ref[:, g * GW:(g + 1) * GW]).astype(BF16)

    ubuf[0:halo, :] = ubuf[L:L + halo, :]


def _ssd(proj, dt, conv_w, conv_b, dt_bias, a_log, d_skip, norm_w):
    bsz, seq, _ = proj.shape
    L = SSD_CHUNK
    pad = LANES - SSD_HEADS
    dtb = jnp.pad(dt_bias, (0, pad)).reshape(1, LANES)
    alog = jnp.pad(a_log, (0, pad)).reshape(1, LANES)
    dskip_e = jnp.repeat(d_skip, SSD_HEAD_DIM).reshape(1, SSD_D_INNER)
    head_of_col = np.arange(SSD_D_INNER) // SSD_HEAD_DIM
    e = (np.arange(LANES)[:, None] == head_of_col[None, :]).astype(np.float32)
    e2 = jnp.asarray(np.concatenate([e, e], axis=0), dtype=BF16)
    const = lambda shape: pl.BlockSpec(shape, lambda b, c: (0,) * len(shape))
    return pl.pallas_call(
        _ssd_kernel,
        grid=(bsz, seq // L),
        in_specs=[
            pl.BlockSpec((1, L, SSD_CONV_DIM), lambda b, c: (b, c, COL_XBC // SSD_CONV_DIM)),
            pl.BlockSpec((1, L, SSD_D_INNER), lambda b, c: (b, c, COL_Z // SSD_D_INNER)),
            pl.BlockSpec((1, L, LANES), lambda b, c: (b, c, 0)),
            const((SSD_CONV, SSD_CONV_DIM)),
            const((1, SSD_CONV_DIM)),
            const((1, LANES)),
            const((1, LANES)),
            const((1, SSD_D_INNER)),
            const((1, SSD_D_INNER)),
            const((2 * LANES, SSD_D_INNER)),
        ],
        out_specs=pl.BlockSpec((1, L, SSD_D_INNER), lambda b, c: (b, c, 0)),
        out_shape=jax.ShapeDtypeStruct((bsz, seq, SSD_D_INNER), BF16),
        scratch_shapes=[
            pltpu.VMEM((L + 2 * SUBLANES, SSD_CONV_DIM), F32),
            pltpu.VMEM((SSD_GROUPS, SSD_D_STATE, SSD_GROUP_WIDTH), F32),
        ],
        compiler_params=pltpu.CompilerParams(
            dimension_semantics=("parallel", "arbitrary"),
            vmem_limit_bytes=48 * 1024 * 1024),
        name="ssd",
    )(proj, proj, dt, conv_w, conv_b.reshape(1, SSD_CONV_DIM), dtb, alog, dskip_e,
      norm_w.reshape(1, SSD_D_INNER), e2)


def _layer_norm(r, g, b):
    mu = jnp.mean(r, axis=1, keepdims=True)
    rc = r - mu
    var = jnp.mean(rc * rc, axis=1, keepdims=True)
    return rc * lax.rsqrt(var + LN_EPS) * g + b


def _mix_kernel(ysb_ref, yssd_ref, g_ref, x_ref, gate1_ref, wsb_ref, wssd_ref, wout_ref, lng_ref, lnb_ref,
                sc2_ref, sh2_ref, wr_hi_ref, wr_lo_ref, br_ref, tri_ref,
                x1_ref, route_ref, cnt_out_ref, cnt_ref):
    first = jnp.logical_and(pl.program_id(0) == 0, pl.program_id(1) == 0)

    @pl.when(first)
    def _():
        cnt_ref[...] = jnp.zeros_like(cnt_ref)

    tm = x_ref.shape[1]
    a_sb = _dot(ysb_ref[0], wsb_ref[...])
    a_ssd = _dot(yssd_ref[0], wssd_ref[...])
    gates = g_ref[0].astype(F32)
    merged = _sigmoid(gates[:, :D_MODEL]) * a_sb + _sigmoid(gates[:, D_MODEL:]) * a_ssd
    y = _dot(merged.astype(BF16), wout_ref[...])
    x1 = _layer_norm(DEEPNORM_ALPHA * x_ref[0] + gate1_ref[0] * y, lng_ref[...], lnb_ref[...])
    x1_ref[0] = x1

    h2 = x1 * (1.0 + sc2_ref[0]) + sh2_ref[0]
    h_1, h_2, _ = _split3(h2)
    logits = (_dot(h_1, wr_hi_ref[...]) + _dot(h_2, wr_hi_ref[...]) + _dot(h_1, wr_lo_ref[...])
              + br_ref[...])
    lane = lax.broadcasted_iota(jnp.int32, (tm, LANES), 1).astype(F32)
    neg = -jnp.inf
    big = float(LANES)
    gl = jnp.where(lane < MOE_GROUPS, logits, neg)
    gmax = jnp.max(gl, axis=1, keepdims=True)
    g_prob = 1.0 / jnp.sum(jnp.exp(gl - gmax), axis=1, keepdims=True)
    g_idx = jnp.min(jnp.where(gl == gmax, lane, big), axis=1, keepdims=True)
    lo = MOE_GROUPS + MOE_EXPERTS_PER_GROUP * g_idx
    l1 = jnp.where((lane >= lo) & (lane < lo + MOE_EXPERTS_PER_GROUP), logits, neg)
    m1 = jnp.max(l1, axis=1, keepdims=True)
    i1 = jnp.min(jnp.where(l1 == m1, lane, big), axis=1, keepdims=True)
    l2 = jnp.where(lane == i1, neg, l1)
    m2 = jnp.max(l2, axis=1, keepdims=True)
    i2 = jnp.min(jnp.where(l2 == m2, lane, big), axis=1, keepdims=True)
    e21 = jnp.exp(m2 - m1)
    w0 = g_prob / (1.0 + e21)
    w1 = g_prob * e21 / (1.0 + e21)
    e0 = i1 - MOE_GROUPS
    e1 = i2 - MOE_GROUPS

    sel0 = lane == e0
    sel1 = lane == e1
    onehot = jnp.where(sel0 | sel1, 1.0, 0.0)
    before = _dot(tri_ref[...], onehot.astype(BF16)) + cnt_ref[...]
    rank0 = jnp.sum(jnp.where(sel0, before, 0.0), axis=1, keepdims=True)
    rank1 = jnp.sum(jnp.where(sel1, before, 0.0), axis=1, keepdims=True)
    cnt_new = cnt_ref[...] + jnp.sum(onehot, axis=0, keepdims=True)
    cnt_ref[...] = cnt_new
    cnt_out_ref[...] = jnp.broadcast_to(cnt_new, cnt_out_ref.shape)

    route = jnp.zeros((tm, LANES), F32)
    for idx, val in ((ROUTE_E0, e0), (ROUTE_E1, e1), (ROUTE_R0, rank0), (ROUTE_R1, rank1),
                     (ROUTE_W0, w0), (ROUTE_W1, w1)):
        route = jnp.where(lane == float(idx), val, route)
    route_ref[0] = route


def _mix_out(y_sb, y_ssd, proj, x, gate1, scale2, shift2, w_sb, w_ssd, w_out, ln_g, ln_b, wr_hi, wr_lo, br):
    bsz, seq, _ = x.shape
    tm = min(MIX_TM, seq)
    tri = jnp.asarray(np.tril(np.ones((tm, tm), np.float32), -1), dtype=BF16)
    const = lambda shape: pl.BlockSpec(shape, lambda b, i: (0,) * len(shape))
    per_batch = pl.BlockSpec((1, 1, D_MODEL), lambda b, i: (b, 0, 0))
    return pl.pallas_call(
        _mix_kernel,
        grid=(bsz, seq // tm),
        in_specs=[
            pl.BlockSpec((1, tm, SB_WIDTH), lambda b, i: (b, i, 0)),
            pl.BlockSpec((1, tm, SSD_D_INNER), lambda b, i: (b, i, 0)),
            pl.BlockSpec((1, tm, 2 * D_MODEL), lambda b, i: (b, i, COL_G // (2 * D_MODEL))),
            pl.BlockSpec((1, tm, D_MODEL), lambda b, i: (b, i, 0)),
            per_batch,
            const((SB_WIDTH, D_MODEL)),
            const((SSD_D_INNER, D_MODEL)),
            const((D_MODEL, D_MODEL)),
            const((1, D_MODEL)),
            const((1, D_MODEL)),
            per_batch,
            per_batch,
            const((D_MODEL, LANES)),
            const((D_MODEL, LANES)),
            const((1, LANES)),
            const((tm, tm)),
        ],
        out_specs=[
            pl.BlockSpec((1, tm, D_MODEL), lambda b, i: (b, i, 0)),
            pl.BlockSpec((1, tm, LANES), lambda b, i: (b, i, 0)),
            pl.BlockSpec((SUBLANES, LANES), lambda b, i: (0, 0)),
        ],
        out_shape=[
            jax.ShapeDtypeStruct((bsz, seq, D_MODEL), F32),
            jax.ShapeDtypeStruct((bsz, seq, LANES), F32),
            jax.ShapeDtypeStruct((SUBLANES, LANES), F32),
        ],
        scratch_shapes=[pltpu.VMEM((1, LANES), F32)],
        compiler_params=pltpu.CompilerParams(
            dimension_semantics=("arbitrary", "arbitrary"),
            vmem_limit_bytes=48 * 1024 * 1024),
        name="mix_out",
    )(y_sb, y_ssd, proj, x, gate1, w_sb, w_ssd, w_out, ln_g.reshape(1, D_MODEL), ln_b.reshape(1, D_MODEL),
      scale2, shift2, wr_hi, wr_lo, br, tri)


def _idx_copy(pos_ref, idx_ref, sem_ref, step, slot, n):
    return pltpu.make_async_copy(pos_ref.at[pl.ds(pl.multiple_of(step * n, n), n)], idx_ref.at[slot],
                                 sem_ref.at[slot])


def _dispatch_kernel(x1_ref, sc2_ref, sh2_ref, pos_ref, xs_in_ref, xs_ref, hbuf, idx_ref, idx_sem, row_sem):
    del xs_in_ref
    tm = x1_ref.shape[1]
    n = 2 * tm
    step = pl.program_id(0) * pl.num_programs(1) + pl.program_id(1)
    nsteps = pl.num_programs(0) * pl.num_programs(1)
    slot = step % 2

    @pl.when(step == 0)
    def _():
        _idx_copy(pos_ref, idx_ref, idx_sem, 0, 0, n).start()

    _idx_copy(pos_ref, idx_ref, idx_sem, step, slot, n).wait()

    @pl.when(step + 1 < nsteps)
    def _():
        _idx_copy(pos_ref, idx_ref, idx_sem, step + 1, 1 - slot, n).start()

    hbuf[...] = x1_ref[0] * (1.0 + sc2_ref[0]) + sh2_ref[0]

    def row_copy(i, k):
        p = idx_ref[slot, 2 * i + k]
        return pltpu.make_async_copy(hbuf.at[pl.ds(i, 1)], xs_ref.at[pl.ds(p, 1)], row_sem)

    def start(i, carry):
        row_copy(i, 0).start()
        row_copy(i, 1).start()
        return carry

    lax.fori_loop(0, tm, start, 0)

    def wait(i, carry):
        row_copy(i, 0).wait()
        row_copy(i, 1).wait()
        return carry

    lax.fori_loop(0, tm, wait, 0)


def _dispatch(x1, scale2, shift2, pos_flat, n_rows):
    bsz, seq, _ = x1.shape
    tm = min(MOE_TM, seq)
    xs0 = jnp.zeros((n_rows, D_MODEL), F32)
    per_batch = pl.BlockSpec((1, 1, D_MODEL), lambda b, i: (b, 0, 0))
    return pl.pallas_call(
        _dispatch_kernel,
        grid=(bsz, seq // tm),
        in_specs=[
            pl.BlockSpec((1, tm, D_MODEL), lambda b, i: (b, i, 0)),
            per_batch,
            per_batch,
            pl.BlockSpec(memory_space=pl.ANY),
            pl.BlockSpec(memory_space=pl.ANY),
        ],
        out_specs=pl.BlockSpec(memory_space=pl.ANY),
        out_shape=jax.ShapeDtypeStruct((n_rows, D_MODEL), F32),
        scratch_shapes=[
            pltpu.VMEM((tm, D_MODEL), F32),
            pltpu.SMEM((2, 2 * tm), jnp.int32),
            pltpu.SemaphoreType.DMA((2,)),
            pltpu.SemaphoreType.DMA(()),
        ],
        input_output_aliases={4: 0},
        compiler_params=pltpu.CompilerParams(dimension_semantics=("arbitrary", "arbitrary")),
        name="dispatch",
    )(x1, scale2, shift2, pos_flat, xs0)


def _experts_kernel(te_ref, tv_ref, xs_ref, wg_ref, wu_ref, wd_ref, y_ref):
    i = pl.program_id(0)

    @pl.when(tv_ref[i] > 0)
    def _():
        xb = xs_ref[...].astype(BF16)
        gate = _dot(xb, wg_ref[0])
        up = _dot(xb, wu_ref[0])
        hid = (_silu(gate) * up).astype(BF16)
        y_ref[...] = _dot(hid, wd_ref[0])

    @pl.when(tv_ref[i] == 0)
    def _():
        y_ref[...] = jnp.zeros_like(y_ref)


def _experts(xs, tile_expert, tile_valid, wg, wu, wd):
    n_rows = xs.shape[0]
    tm = MOE_TM
    grid_spec = pltpu.PrefetchScalarGridSpec(
        num_scalar_prefetch=2,
        grid=(n_rows // tm,),
        in_specs=[
            pl.BlockSpec((tm, D_MODEL), lambda i, te, tv: (i, 0)),
            pl.BlockSpec((1, D_MODEL, MOE_D_FF), lambda i, te, tv: (te[i], 0, 0)),
            pl.BlockSpec((1, D_MODEL, MOE_D_FF), lambda i, te, tv: (te[i], 0, 0)),
            pl.BlockSpec((1, MOE_D_FF, D_MODEL), lambda i, te, tv: (te[i], 0, 0)),
        ],
        out_specs=pl.BlockSpec((tm, D_MODEL), lambda i, te, tv: (i, 0)),
    )
    return pl.pallas_call(
        _experts_kernel,
        grid_spec=grid_spec,
        out_shape=jax.ShapeDtypeStruct((n_rows, D_MODEL), F32),
        compiler_params=pltpu.CompilerParams(
            dimension_semantics=("arbitrary",),
            vmem_limit_bytes=48 * 1024 * 1024),
        name="experts",
    )(tile_expert, tile_valid, xs, wg, wu, wd)


def _combine_kernel(x1_ref, route_ref, gate2_ref, lng_ref, lnb_ref, pos_ref, y_ref, o_ref,
                    ybuf, idx_ref, idx_sem, row_sem):
    tm = x1_ref.shape[1]
    n = 2 * tm
    step = pl.program_id(0) * pl.num_programs(1) + pl.program_id(1)
    nsteps = pl.num_programs(0) * pl.num_programs(1)
    slot = step % 2

    @pl.when(step == 0)
    def _():
        _idx_copy(pos_ref, idx_ref, idx_sem, 0, 0, n).start()

    _idx_copy(pos_ref, idx_ref, idx_sem, step, slot, n).wait()

    @pl.when(step + 1 < nsteps)
    def _():
        _idx_copy(pos_ref, idx_ref, idx_sem, step + 1, 1 - slot, n).start()

    def row_copy(i, k):
        p = idx_ref[slot, 2 * i + k]
        return pltpu.make_async_copy(y_ref.at[pl.ds(p, 1)], ybuf.at[k, pl.ds(i, 1)], row_sem)

    def start(i, carry):
        row_copy(i, 0).start()
        row_copy(i, 1).start()
        return carry

    lax.fori_loop(0, tm, start, 0)

    def wait(i, carry):
        row_copy(i, 0).wait()
        row_copy(i, 1).wait()
        return carry

    lax.fori_loop(0, tm, wait, 0)

    route = route_ref[0]
    lane = lax.broadcasted_iota(jnp.int32, route.shape, 1)
    w0 = jnp.sum(jnp.where(lane == ROUTE_W0, route, 0.0), axis=1, keepdims=True)
    w1 = jnp.sum(jnp.where(lane == ROUTE_W1, route, 0.0), axis=1, keepdims=True)
    y = w0 * ybuf[0] + w1 * ybuf[1]
    o_ref[0] = _layer_norm(DEEPNORM_ALPHA * x1_ref[0] + gate2_ref[0] * y, lng_ref[...], lnb_ref[...])


def _combine_ln(x1, route, gate2, ln_g, ln_b, pos_flat, y_rows):
    bsz, seq, _ = x1.shape
    tm = min(MOE_TM, seq)
    const = lambda shape: pl.BlockSpec(shape, lambda b, i: (0,) * len(shape))
    return pl.pallas_call(
        _combine_kernel,
        grid=(bsz, seq // tm),
        in_specs=[
            pl.BlockSpec((1, tm, D_MODEL), lambda b, i: (b, i, 0)),
            pl.BlockSpec((1, tm, LANES), lambda b, i: (b, i, 0)),
            pl.BlockSpec((1, 1, D_MODEL), lambda b, i: (b, 0, 0)),
            const((1, D_MODEL)),
            const((1, D_MODEL)),
            pl.BlockSpec(memory_space=pl.ANY),
            pl.BlockSpec(memory_space=pl.ANY),
        ],
        out_specs=pl.BlockSpec((1, tm, D_MODEL), lambda b, i: (b, i, 0)),
        out_shape=jax.ShapeDtypeStruct((bsz, seq, D_MODEL), F32),
        scratch_shapes=[
            pltpu.VMEM((2, tm, D_MODEL), F32),
            pltpu.SMEM((2, 2 * tm), jnp.int32),
            pltpu.SemaphoreType.DMA((2,)),
            pltpu.SemaphoreType.DMA(()),
        ],
        compiler_params=pltpu.CompilerParams(dimension_semantics=("arbitrary", "arbitrary")),
        name="combine_ln",
    )(x1, route, gate2, ln_g.reshape(1, D_MODEL), ln_b.reshape(1, D_MODEL), pos_flat, y_rows)


def _route_plan(route, counts, n_tokens):
    tm = MOE_TM
    flat = route.reshape(n_tokens, LANES)
    expert = flat[:, ROUTE_E0:ROUTE_E1 + 1].astype(jnp.int32)
    rank = flat[:, ROUTE_R0:ROUTE_R1 + 1].astype(jnp.int32)
    cnt = counts[0, :MOE_EXPERTS].astype(jnp.int32)
    padded = ((cnt + tm - 1) // tm) * tm
    seg_end = jnp.cumsum(padded)
    seg_start = seg_end - padded
    pos = (seg_start[expert] + rank).reshape(-1)
    n_rows = 2 * n_tokens + MOE_EXPERTS * tm
    tile_row = jnp.arange(n_rows // tm, dtype=jnp.int32) * tm
    tile_expert = jnp.minimum(jnp.searchsorted(seg_end, tile_row, side="right"), MOE_EXPERTS - 1)
    tile_valid = (tile_row < seg_end[-1]).astype(jnp.int32)
    return pos, tile_expert.astype(jnp.int32), tile_valid, n_rows


def _layer(x, c, w_ada, b_ada, w_in, conv_w, conv_b, dt_bias, a_log, d_skip, ssd_norm_w, w_sb_out, w_ssd_out,
           w_out, ln1_g, ln1_b, w_group, b_group, w_router, b_router, w_gate_e, w_up_e, w_down_e, ln2_g, ln2_b):
    bsz, seq, _ = x.shape
    mod = _ada_mod(c, w_ada, b_ada)
    shift1, scale1, gate1, shift2, scale2, gate2 = [
        m.reshape(bsz, 1, D_MODEL) for m in jnp.split(mod, 6, axis=-1)]

    o_q, o_k, o_v, o_z = 0, SB_WIDTH, 2 * SB_WIDTH, 3 * SB_WIDTH
    o_xbc = o_z + SSD_D_INNER
    o_dt = o_xbc + SSD_CONV_DIM
    o_gsb = o_dt + SSD_HEADS
    o_gssd = o_gsb + D_MODEL
    cols = lambda start, width: w_in[:, start:start + width]
    w_cat = jnp.concatenate([
        cols(o_xbc, SSD_CONV_DIM), cols(o_z, SSD_D_INNER), cols(o_gsb, D_MODEL), cols(o_gssd, D_MODEL),
        cols(o_q, SB_WIDTH), cols(o_k, SB_WIDTH), cols(o_v, SB_WIDTH)], axis=1).astype(BF16)
    w_dt = jnp.pad(cols(o_dt, SSD_HEADS), ((0, 0), (0, LANES - SSD_HEADS)))
    wdt_hi = w_dt.astype(BF16)
    wdt_lo = (w_dt - wdt_hi.astype(F32)).astype(BF16)

    proj, dt = _in_proj(x, scale1, shift1, w_cat, wdt_hi, wdt_lo)
    y_sb = _sb_attention(proj)
    y_ssd = _ssd(proj, dt, conv_w, conv_b, dt_bias, a_log, d_skip, ssd_norm_w)

    w_r = jnp.pad(jnp.concatenate([w_group, w_router], axis=1),
                  ((0, 0), (0, LANES - MOE_GROUPS - MOE_EXPERTS)))
    wr_hi = w_r.astype(BF16)
    wr_lo = (w_r - wr_hi.astype(F32)).astype(BF16)
    b_r = jnp.pad(jnp.concatenate([b_group, b_router]), (0, LANES - MOE_GROUPS - MOE_EXPERTS)).reshape(1, LANES)
    x1, route, counts = _mix_out(y_sb, y_ssd, proj, x, gate1, scale2, shift2,
                                 w_sb_out.astype(BF16), w_ssd_out.astype(BF16), w_out.astype(BF16),
                                 ln1_g, ln1_b, wr_hi, wr_lo, b_r)

    pos, tile_expert, tile_valid, n_rows = _route_plan(route, counts, bsz * seq)
    xs = _dispatch(x1, scale2, shift2, pos, n_rows)
    y_rows = _experts(xs, tile_expert, tile_valid,
                      w_gate_e.astype(BF16), w_up_e.astype(BF16), w_down_e.astype(BF16))
    return _combine_ln(x1, route, gate2, ln2_g, ln2_b, pos, y_rows)


def kernel(x, c, w_ada, b_ada, w_in, conv_w, conv_b, dt_bias, a_log, d_skip, ssd_norm_w, w_sb_out, w_ssd_out,
           w_out, ln1_g, ln1_b, w_group, b_group, w_router, b_router, w_gate_e, w_up_e, w_down_e, ln2_g, ln2_b):
    assert w_ada.shape[0] == DEPTH
    for l in range(DEPTH):
        x = _layer(x, c, w_ada[l], b_ada[l], w_in[l], conv_w[l], conv_b[l], dt_bias[l], a_log[l], d_skip[l],
                   ssd_norm_w[l], w_sb_out[l], w_ssd_out[l], w_out[l], ln1_g[l], ln1_b[l], w_group[l],
                   b_group[l], w_router[l], b_router[l], w_gate_e[l], w_up_e[l], w_down_e[l], ln2_g[l],
                   ln2_b[l])
    return x
```

```python
import functools

import jax
import jax.numpy as jnp
import numpy as np
from jax import lax
from jax.experimental import pallas as pl
from jax.experimental.pallas import tpu as pltpu

F32 = jnp.float32
BF16 = jnp.bfloat16

D_MODEL = 1024
SB_HEADS = 16
SB_HEAD_DIM = 64
SB_WIDTH = SB_HEADS * SB_HEAD_DIM
SSD_D_INNER = 2 * D_MODEL
SSD_HEAD_DIM = 64
SSD_HEADS = SSD_D_INNER // SSD_HEAD_DIM
SSD_GROUPS = 8
SSD_HEADS_PER_GROUP = SSD_HEADS // SSD_GROUPS
SSD_D_STATE = 128
SSD_CONV = 4
SSD_CHUNK = 128
SSD_GROUP_WIDTH = SSD_D_INNER // SSD_GROUPS
SSD_CONV_DIM = SSD_D_INNER + 2 * SSD_GROUPS * SSD_D_STATE
MOE_GROUPS = 4
MOE_EXPERTS_PER_GROUP = 8
MOE_EXPERTS = MOE_GROUPS * MOE_EXPERTS_PER_GROUP
MOE_D_FF = 512
DEPTH = 1
DEEPNORM_ALPHA = (2 * DEPTH) ** 0.25
LN_EPS = 1e-5
RMS_EPS = 1e-5

LANES = 128
SUBLANES = 8

COL_XBC = 0
COL_Z = COL_XBC + SSD_CONV_DIM
COL_G = COL_Z + SSD_D_INNER
COL_Q = COL_G + 2 * D_MODEL
COL_K = COL_Q + SB_WIDTH
COL_V = COL_K + SB_WIDTH
PROJ_COLS = COL_V + SB_WIDTH

IN_TM = 1024
IN_TN = 2816
SB_TQ = 256
MIX_TM = 512
MOE_TM = 512
SB_LOG_WEIGHT_FLOOR = -110.0

ROUTE_E0, ROUTE_E1, ROUTE_R0, ROUTE_R1, ROUTE_W0, ROUTE_W1 = range(6)


def _split3(a):
    a1 = a.astype(BF16)
    r1 = a - a1.astype(F32)
    a2 = r1.astype(BF16)
    a3 = (r1 - a2.astype(F32)).astype(BF16)
    return a1, a2, a3


def _dot(a, b):
    return jnp.dot(a, b, preferred_element_type=F32)


def _dot_nt(a, b):
    return lax.dot_general(a, b, (((1,), (1,)), ((), ())), preferred_element_type=F32)


def _sigmoid(x):
    return 1.0 / (1.0 + jnp.exp(-x))


def _silu(x):
    return x * _sigmoid(x)


def _softplus(x):
    return jnp.maximum(x, 0.0) + jnp.log(1.0 + jnp.exp(-jnp.abs(x)))


def _ada_kernel(c_ref, w_ref, b_ref, o_ref):
    c = c_ref[...]
    o_ref[...] = _dot(_silu(c), w_ref[...]) + b_ref[...]


def _ada_mod(c, w_ada, b_ada):
    bsz = c.shape[0]
    n = w_ada.shape[1]
    tn = D_MODEL
    return pl.pallas_call(
        _ada_kernel,
        grid=(n // tn,),
        in_specs=[
            pl.BlockSpec((bsz, D_MODEL), lambda j: (0, 0)),
            pl.BlockSpec((D_MODEL, tn), lambda j: (0, j)),
            pl.BlockSpec((1, tn), lambda j: (0, j)),
        ],
        out_specs=pl.BlockSpec((bsz, tn), lambda j: (0, j)),
        out_shape=jax.ShapeDtypeStruct((bsz, n), F32),
        name="ada_mod",
    )(c, w_ada, b_ada.reshape(1, n))


def _in_proj_kernel(x_ref, sc_ref, sh_ref, w_ref, wdt_hi_ref, wdt_lo_ref, o_ref, dt_ref, h_ref):
    j = pl.program_id(2)

    @pl.when(j == 0)
    def _():
        h = x_ref[0] * (1.0 + sc_ref[0]) + sh_ref[0]
        h1, h2, _ = _split3(h)
        h_ref[...] = h1
        dt_ref[0] = _dot(h1, wdt_hi_ref[...]) + _dot(h2, wdt_hi_ref[...]) + _dot(h1, wdt_lo_ref[...])

    o_ref[0] = _dot(h_ref[...], w_ref[...]).astype(BF16)


def _in_proj(x, scale1, shift1, w_cat, wdt_hi, wdt_lo):
    bsz, seq, _ = x.shape
    tm, tn = min(IN_TM, seq), IN_TN
    return pl.pallas_call(
        _in_proj_kernel,
        grid=(bsz, seq // tm, PROJ_COLS // tn),
        in_specs=[
            pl.BlockSpec((1, tm, D_MODEL), lambda b, i, j: (b, i, 0)),
            pl.BlockSpec((1, 1, D_MODEL), lambda b, i, j: (b, 0, 0)),
            pl.BlockSpec((1, 1, D_MODEL), lambda b, i, j: (b, 0, 0)),
            pl.BlockSpec((D_MODEL, tn), lambda b, i, j: (0, j)),
            pl.BlockSpec((D_MODEL, LANES), lambda b, i, j: (0, 0)),
            pl.BlockSpec((D_MODEL, LANES), lambda b, i, j: (0, 0)),
        ],
        out_specs=[
            pl.BlockSpec((1, tm, tn), lambda b, i, j: (b, i, j)),
            pl.BlockSpec((1, tm, LANES), lambda b, i, j: (b, i, 0)),
        ],
        out_shape=[
            jax.ShapeDtypeStruct((bsz, seq, PROJ_COLS), BF16),
            jax.ShapeDtypeStruct((bsz, seq, LANES), F32),
        ],
        scratch_shapes=[pltpu.VMEM((tm, D_MODEL), BF16)],
        compiler_params=pltpu.CompilerParams(
            dimension_semantics=("parallel", "parallel", "arbitrary"),
            vmem_limit_bytes=48 * 1024 * 1024),
        name="in_proj",
    )(x, scale1, shift1, w_cat, wdt_hi, wdt_lo)


def _sb_kernel(q_ref, k_ref, v_ref, o_ref, acc_ref, r_ref, *, tq, scale):
    qi = pl.program_id(2)
    tk = tq
    fold_scale = float(np.log2(scale)).is_integer()
    q = q_ref[0]
    if fold_scale:
        q = q * scale
    lane = lax.broadcasted_iota(jnp.int32, (tq, LANES), 1)
    row = lax.broadcasted_iota(jnp.int32, (tq, tk), 0)
    col = lax.broadcasted_iota(jnp.int32, (tq, tk), 1)
    strictly_past = col < row
    later_key = (row > col).astype(BF16)

    def head_q(head):
        in_head = (lane >= head * SB_HEAD_DIM) & (lane < (head + 1) * SB_HEAD_DIM)
        return jnp.where(in_head, q, jnp.zeros_like(q))

    def tile(qh, kt, carry, diag):
        start = pl.multiple_of(kt * tk, tk)
        k_blk = k_ref[0, pl.ds(start, tk), :]
        v_blk = v_ref[0, pl.ds(start, tk), :]
        s = _dot_nt(qh, k_blk)
        if not fold_scale:
            s = s * scale
        log_not = jnp.minimum(-s, 0.0) - jnp.log(1.0 + jnp.exp(-jnp.abs(s)))
        if diag:
            log_not = jnp.where(strictly_past, log_not, 0.0)
        between = _dot(log_not.astype(BF16), later_key)
        logw = s + log_not + between
        if carry is not None:
            logw = logw + carry
        w = jnp.exp(logw)
        if diag:
            w = jnp.where(strictly_past, w, 0.0)
        return _dot(w.astype(BF16), v_blk), jnp.sum(log_not, axis=1, keepdims=True)

    def newest_tiles(with_previous):
        tops = []
        for head in range(2):
            qh = head_q(head)
            acc, r = tile(qh, qi, None, True)
            if with_previous:
                acc_p, r_p = tile(qh, qi - 1, r, False)
                acc = acc + acc_p
                r = r + r_p
            acc_ref[head] = acc
            r_ref[head] = jnp.broadcast_to(r, (tq, LANES))
            tops.append(jnp.max(r))
        return jnp.maximum(tops[0], tops[1])

    @pl.when(qi == 0)
    def _():
        newest_tiles(False)

    @pl.when(qi > 0)
    def _():
        top = newest_tiles(True)

        def cond(carry):
            kt, go = carry
            return jnp.logical_and(kt >= 0, go > 0)

        def body(carry):
            kt, _ = carry
            tops = []
            for head in range(2):
                r = r_ref[head]
                carry = jnp.concatenate([r] * (tk // LANES), axis=1)
                acc, r_t = tile(head_q(head), kt, carry, False)
                acc_ref[head] += acc
                r_new = r + r_t
                r_ref[head] = r_new
                tops.append(jnp.max(r_new))
            go = jnp.maximum(tops[0], tops[1]) > SB_LOG_WEIGHT_FLOOR
            return kt - 1, go.astype(jnp.int32)

        lax.while_loop(cond, body, (qi - 2, (top > SB_LOG_WEIGHT_FLOOR).astype(jnp.int32)))

    o_ref[0] = jnp.where(lane < SB_HEAD_DIM, acc_ref[0], acc_ref[1]).astype(BF16)


def _sb_attention(proj):
    bsz, seq, _ = proj.shape
    tq = min(SB_TQ, seq)
    pairs = SB_WIDTH // LANES
    kern = functools.partial(_sb_kernel, tq=tq, scale=SB_HEAD_DIM ** -0.5)
    return pl.pallas_call(
        kern,
        grid=(bsz, pairs, seq // tq),
        in_specs=[
            pl.BlockSpec((1, tq, LANES), lambda b, p, i: (b, i, COL_Q // LANES + p)),
            pl.BlockSpec((1, seq, LANES), lambda b, p, i: (b, 0, COL_K // LANES + p)),
            pl.BlockSpec((1, seq, LANES), lambda b, p, i: (b, 0, COL_V // LANES + p)),
        ],
        out_specs=pl.BlockSpec((1, tq, LANES), lambda b, p, i: (b, i, p)),
        out_shape=jax.ShapeDtypeStruct((bsz, seq, SB_WIDTH), BF16),
        scratch_shapes=[pltpu.VMEM((2, tq, LANES), F32), pltpu.VMEM((2, tq, LANES), F32)],
        compiler_params=pltpu.CompilerParams(
            dimension_semantics=("parallel", "parallel", "arbitrary")),
        name="sb_attn",
    )(proj, proj, proj)


def _ssd_kernel(xbc_ref, z_ref, dt_ref, cw_ref, cb_ref, dtb_ref, alog_ref, dskip_ref, nw_ref, e2_ref,
                o_ref, ubuf, state_ref):
    c = pl.program_id(1)
    L = SSD_CHUNK
    GW = SSD_GROUP_WIDTH
    NS = SSD_D_STATE
    halo = SUBLANES

    @pl.when(c == 0)
    def _():
        state_ref[...] = jnp.zeros_like(state_ref)
        ubuf[0:halo, :] = jnp.zeros((halo, SSD_CONV_DIM), F32)

    ubuf[halo:halo + L, :] = xbc_ref[0].astype(F32)

    def conv(c0, width):
        window = ubuf[:, c0:c0 + width]
        acc = cb_ref[:, c0:c0 + width] + cw_ref[SSD_CONV - 1:SSD_CONV, c0:c0 + width] * window[halo:halo + L]
        for kk in range(SSD_CONV - 1):
            delayed = pltpu.roll(window, SSD_CONV - 1 - kk, axis=0)[halo:halo + L]
            acc = acc + cw_ref[kk:kk + 1, c0:c0 + width] * delayed
        return _silu(acc)

    lane = lax.broadcasted_iota(jnp.int32, (1, LANES), 1)
    head_lane = lane < SSD_HEADS
    dtv = _softplus(dt_ref[0] + dtb_ref[...])
    a = jnp.where(head_lane, -jnp.exp(alog_ref[...]), 0.0)
    adt = a * dtv
    li = lax.broadcasted_iota(jnp.int32, (L, L), 0)
    si = lax.broadcasted_iota(jnp.int32, (L, L), 1)
    causal = si <= li
    tri = causal.astype(BF16)
    a1, a2, a3 = _split3(adt)
    cs = _dot(tri, a1) + _dot(tri, a2) + _dot(tri, a3)
    cs_t = cs.T
    cs_end = cs[L - 1:L, :]
    decay_to_end = jnp.exp(cs_end - cs)
    decay_from_start = jnp.exp(cs)

    def expand(v, g):
        hi = v.astype(BF16)
        lo = (v - hi.astype(F32)).astype(BF16)
        return _dot(jnp.concatenate([hi, lo], axis=1), e2_ref[:, g * GW:(g + 1) * GW])

    lane_gw = lax.broadcasted_iota(jnp.int32, (L, GW), 1)

    for g in range(SSD_GROUPS):
        xs = conv(g * GW, GW)
        bmat = conv(SSD_D_INNER + g * NS, NS).astype(BF16)
        cmat = conv(SSD_D_INNER + SSD_GROUPS * NS + g * NS, NS).astype(BF16)
        cb = _dot_nt(cmat, bmat)
        dt_e = expand(dtv, g)
        xdt = xs * dt_e
        xdt_b = xdt.astype(BF16)
        xdte_b = (xdt * expand(decay_to_end, g)).astype(BF16)
        from_start_e = expand(decay_from_start, g)
        masked_cb, head_x = [], []
        for r in range(SSD_HEADS_PER_GROUP):
            h = g * SSD_HEADS_PER_GROUP + r
            seg = jnp.broadcast_to(cs[:, h:h + 1], (L, L)) - cs_t[h:h + 1, :]
            decay = jnp.exp(jnp.where(causal, seg, -jnp.inf))
            masked_cb.append((cb * decay).astype(BF16))
            in_head = (lane_gw >= r * SSD_HEAD_DIM) & (lane_gw < (r + 1) * SSD_HEAD_DIM)
            head_x.append(jnp.where(in_head, xdt_b, jnp.zeros_like(xdt_b)))
        y = _dot(jnp.concatenate(masked_cb, axis=1), jnp.concatenate(head_x, axis=0))
        state = state_ref[g]
        y = y + _dot(cmat, state.astype(BF16)) * from_start_e
        new_state = lax.dot_general(bmat, xdte_b, (((0,), (0,)), ((), ())), preferred_element_type=F32)
        state_ref[g] = state * from_start_e[L - 1:L, :] + new_state
        y = y + dskip_ref[:, g * GW:(g + 1) * GW] * xs
        zg = z_ref[0, :, g * GW:(g + 1) * GW].astype(F32)
        yg = y * _silu(zg)
        ms = jnp.mean(yg * yg, axis=1, keepdims=True)
        o_ref[0, :, g * GW:(g + 1) * GW] = (
            yg * lax.rsqrt(ms + RMS_EPS) * nw_ref[:, g * GW:(g + 1) * GW]).astype(BF16)

    ubuf[0:halo, :] = ubuf[L:L + halo, :]


def _ssd(proj, dt, conv_w, conv_b, dt_bias, a_log, d_skip, norm_w):
    bsz, seq, _ = proj.shape
    L = SSD_CHUNK
    pad = LANES - SSD_HEADS
    dtb = jnp.pad(dt_bias, (0, pad)).reshape(1, LANES)
    alog = jnp.pad(a_log, (0, pad)).reshape(1, LANES)
    dskip_e = jnp.repeat(d_skip, SSD_HEAD_DIM).reshape(1, SSD_D_INNER)
    head_of_col = np.arange(SSD_D_INNER) // SSD_HEAD_DIM
    e = (np.arange(LANES)[:, None] == head_of_col[None, :]).astype(np.float32)
    e2 = jnp.asarray(np.concatenate([e, e], axis=0), dtype=BF16)
    const = lambda shape: pl.BlockSpec(shape, lambda b, c: (0,) * len(shape))
    return pl.pallas_call(
        _ssd_kernel,
        grid=(bsz, seq // L),
        in_specs=[
            pl.BlockSpec((1, L, SSD_CONV_DIM), lambda b, c: (b, c, COL_XBC // SSD_CONV_DIM)),
            pl.BlockSpec((1, L, SSD_D_INNER), lambda b, c: (b, c, COL_Z // SSD_D_INNER)),
            pl.BlockSpec((1, L, LANES), lambda b, c: (b, c, 0)),
            const((SSD_CONV, SSD_CONV_DIM)),
            const((1, SSD_CONV_DIM)),
            const((1, LANES)),
            const((1, LANES)),
            const((1, SSD_D_INNER)),
            const((1, SSD_D_INNER)),
            const((2 * LANES, SSD_D_INNER)),
        ],
        out_specs=pl.BlockSpec((1, L, SSD_D_INNER), lambda b, c: (b, c, 0)),
        out_shape=jax.ShapeDtypeStruct((bsz, seq, SSD_D_INNER), BF16),
        scratch_shapes=[
            pltpu.VMEM((L + SUBLANES, SSD_CONV_DIM), F32),
            pltpu.VMEM((SSD_GROUPS, SSD_D_STATE, SSD_GROUP_WIDTH), F32),
        ],
        compiler_params=pltpu.CompilerParams(
            dimension_semantics=("parallel", "arbitrary"),
            vmem_limit_bytes=48 * 1024 * 1024),
        name="ssd",
    )(proj, proj, dt, conv_w, conv_b.reshape(1, SSD_CONV_DIM), dtb, alog, dskip_e,
      norm_w.reshape(1, SSD_D_INNER), e2)


def _layer_norm(r, g, b):
    mu = jnp.mean(r, axis=1, keepdims=True)
    rc = r - mu
    var = jnp.mean(rc * rc, axis=1, keepdims=True)
    return rc * lax.rsqrt(var + LN_EPS) * g + b


def _mix_kernel(ysb_ref, yssd_ref, g_ref, x_ref, gate1_ref, wsb_ref, wssd_ref, wout_ref, lng_ref, lnb_ref,
                sc2_ref, sh2_ref, wr_hi_ref, wr_lo_ref, br_ref, tri_ref,
                x1_ref, route_ref, cnt_out_ref, cnt_ref):
    first = jnp.logical_and(pl.program_id(0) == 0, pl.program_id(1) == 0)

    @pl.when(first)
    def _():
        cnt_ref[...] = jnp.zeros_like(cnt_ref)

    tm = x_ref.shape[1]
    a_sb = _dot(ysb_ref[0], wsb_ref[...])
    a_ssd = _dot(yssd_ref[0], wssd_ref[...])
    gates = g_ref[0].astype(F32)
    merged = _sigmoid(gates[:, :D_MODEL]) * a_sb + _sigmoid(gates[:, D_MODEL:]) * a_ssd
    y = _dot(merged.astype(BF16), wout_ref[...])
    x1 = _layer_norm(DEEPNORM_ALPHA * x_ref[0] + gate1_ref[0] * y, lng_ref[...], lnb_ref[...])
    x1_ref[0] = x1

    h2 = x1 * (1.0 + sc2_ref[0]) + sh2_ref[0]
    h_1, h_2, _ = _split3(h2)
    logits = (_dot(h_1, wr_hi_ref[...]) + _dot(h_2, wr_hi_ref[...]) + _dot(h_1, wr_lo_ref[...])
              + br_ref[...])
    lane = lax.broadcasted_iota(jnp.int32, (tm, LANES), 1).astype(F32)
    neg = -jnp.inf
    big = float(LANES)
    gl = jnp.where(lane < MOE_GROUPS, logits, neg)
    gmax = jnp.max(gl, axis=1, keepdims=True)
    g_prob = 1.0 / jnp.sum(jnp.exp(gl - gmax), axis=1, keepdims=True)
    g_idx = jnp.min(jnp.where(gl == gmax, lane, big), axis=1, keepdims=True)
    lo = MOE_GROUPS + MOE_EXPERTS_PER_GROUP * g_idx
    l1 = jnp.where((lane >= lo) & (lane < lo + MOE_EXPERTS_PER_GROUP), logits, neg)
    m1 = jnp.max(l1, axis=1, keepdims=True)
    i1 = jnp.min(jnp.where(l1 == m1, lane, big), axis=1, keepdims=True)
    l2 = jnp.where(lane == i1, neg, l1)
    m2 = jnp.max(l2, axis=1, keepdims=True)
    i2 = jnp.min(jnp.where(l2 == m2, lane, big), axis=1, keepdims=True)
    e21 = jnp.exp(m2 - m1)
    w0 = g_prob / (1.0 + e21)
    w1 = g_prob * e21 / (1.0 + e21)
    e0 = i1 - MOE_GROUPS
    e1 = i2 - MOE_GROUPS

    sel0 = lane == e0
    sel1 = lane == e1
    onehot = jnp.where(sel0 | sel1, 1.0, 0.0)
    before = _dot(tri_ref[...], onehot.astype(BF16)) + cnt_ref[...]
    rank0 = jnp.sum(jnp.where(sel0, before, 0.0), axis=1, keepdims=True)
    rank1 = jnp.sum(jnp.where(sel1, before, 0.0), axis=1, keepdims=True)
    cnt_new = cnt_ref[...] + jnp.sum(onehot, axis=0, keepdims=True)
    cnt_ref[...] = cnt_new
    cnt_out_ref[...] = jnp.broadcast_to(cnt_new, cnt_out_ref.shape)

    route = jnp.zeros((tm, LANES), F32)
    for idx, val in ((ROUTE_E0, e0), (ROUTE_E1, e1), (ROUTE_R0, rank0), (ROUTE_R1, rank1),
                     (ROUTE_W0, w0), (ROUTE_W1, w1)):
        route = jnp.where(lane == float(idx), val, route)
    route_ref[0] = route


def _mix_out(y_sb, y_ssd, proj, x, gate1, scale2, shift2, w_sb, w_ssd, w_out, ln_g, ln_b, wr_hi, wr_lo, br):
    bsz, seq, _ = x.shape
    tm = min(MIX_TM, seq)
    tri = jnp.asarray(np.tril(np.ones((tm, tm), np.float32), -1), dtype=BF16)
    const = lambda shape: pl.BlockSpec(shape, lambda b, i: (0,) * len(shape))
    per_batch = pl.BlockSpec((1, 1, D_MODEL), lambda b, i: (b, 0, 0))
    return pl.pallas_call(
        _mix_kernel,
        grid=(bsz, seq // tm),
        in_specs=[
            pl.BlockSpec((1, tm, SB_WIDTH), lambda b, i: (b, i, 0)),
            pl.BlockSpec((1, tm, SSD_D_INNER), lambda b, i: (b, i, 0)),
            pl.BlockSpec((1, tm, 2 * D_MODEL), lambda b, i: (b, i, COL_G // (2 * D_MODEL))),
            pl.BlockSpec((1, tm, D_MODEL), lambda b, i: (b, i, 0)),
            per_batch,
            const((SB_WIDTH, D_MODEL)),
            const((SSD_D_INNER, D_MODEL)),
            const((D_MODEL, D_MODEL)),
            const((1, D_MODEL)),
            const((1, D_MODEL)),
            per_batch,
            per_batch,
            const((D_MODEL, LANES)),
            const((D_MODEL, LANES)),
            const((1, LANES)),
            const((tm, tm)),
        ],
        out_specs=[
            pl.BlockSpec((1, tm, D_MODEL), lambda b, i: (b, i, 0)),
            pl.BlockSpec((1, tm, LANES), lambda b, i: (b, i, 0)),
            pl.BlockSpec((SUBLANES, LANES), lambda b, i: (0, 0)),
        ],
        out_shape=[
            jax.ShapeDtypeStruct((bsz, seq, D_MODEL), F32),
            jax.ShapeDtypeStruct((bsz, seq, LANES), F32),
            jax.ShapeDtypeStruct((SUBLANES, LANES), F32),
        ],
        scratch_shapes=[pltpu.VMEM((1, LANES), F32)],
        compiler_params=pltpu.CompilerParams(
            dimension_semantics=("arbitrary", "arbitrary"),
            vmem_limit_bytes=48 * 1024 * 1024),
        name="mix_out",
    )(y_sb, y_ssd, proj, x, gate1, w_sb, w_ssd, w_out, ln_g.reshape(1, D_MODEL), ln_b.reshape(1, D_MODEL),
      scale2, shift2, wr_hi, wr_lo, br, tri)


def _idx_copy(pos_ref, idx_ref, sem_ref, step, slot, n):
    return pltpu.make_async_copy(pos_ref.at[pl.ds(pl.multiple_of(step * n, n), n)],
                                 idx_ref.at[pl.ds(pl.multiple_of(slot * n, n), n)], sem_ref.at[slot])


def _for_each_row(tm, fn):
    def body(io, carry):
        base = pl.multiple_of(io * SUBLANES, SUBLANES)
        for j in range(SUBLANES):
            fn(base + j)
        return carry

    lax.fori_loop(0, tm // SUBLANES, body, 0)


def _dispatch_kernel(x1_ref, sc2_ref, sh2_ref, pos_ref, xs_in_ref, xs_ref, hbuf, idx_ref, idx_sem, row_sem):
    del xs_in_ref
    tm = x1_ref.shape[1]
    n = 2 * tm
    step = pl.program_id(0) * pl.num_programs(1) + pl.program_id(1)
    nsteps = pl.num_programs(0) * pl.num_programs(1)
    slot = step % 2

    def wait_rows(s):
        for _ in range(2):
            pltpu.make_async_copy(hbuf.at[s], xs_ref.at[pl.ds(0, tm)], row_sem.at[s]).wait()

    @pl.when(step == 0)
    def _():
        _idx_copy(pos_ref, idx_ref, idx_sem, 0, 0, n).start()

    _idx_copy(pos_ref, idx_ref, idx_sem, step, slot, n).wait()

    @pl.when(step + 1 < nsteps)
    def _():
        _idx_copy(pos_ref, idx_ref, idx_sem, step + 1, 1 - slot, n).start()

    hbuf[slot] = x1_ref[0] * (1.0 + sc2_ref[0]) + sh2_ref[0]

    def start_row(i):
        for k in range(2):
            p = idx_ref[slot * n + 2 * i + k]
            pltpu.make_async_copy(hbuf.at[slot, pl.ds(i, 1)], xs_ref.at[pl.ds(p, 1)], row_sem.at[slot]).start()

    _for_each_row(tm, start_row)

    @pl.when(step > 0)
    def _():
        wait_rows(1 - slot)

    @pl.when(step + 1 == nsteps)
    def _():
        wait_rows(slot)


def _dispatch(x1, scale2, shift2, pos_flat, n_rows):
    bsz, seq, _ = x1.shape
    tm = min(MOE_TM, seq)
    xs0 = jnp.zeros((n_rows, D_MODEL), F32)
    per_batch = pl.BlockSpec((1, 1, D_MODEL), lambda b, i: (b, 0, 0))
    return pl.pallas_call(
        _dispatch_kernel,
        grid=(bsz, seq // tm),
        in_specs=[
            pl.BlockSpec((1, tm, D_MODEL), lambda b, i: (b, i, 0)),
            per_batch,
            per_batch,
            pl.BlockSpec(memory_space=pl.ANY),
            pl.BlockSpec(memory_space=pl.ANY),
        ],
        out_specs=pl.BlockSpec(memory_space=pl.ANY),
        out_shape=jax.ShapeDtypeStruct((n_rows, D_MODEL), F32),
        scratch_shapes=[
            pltpu.VMEM((2, tm, D_MODEL), F32),
            pltpu.SMEM((4 * tm,), jnp.int32),
            pltpu.SemaphoreType.DMA((2,)),
            pltpu.SemaphoreType.DMA((2,)),
        ],
        input_output_aliases={4: 0},
        compiler_params=pltpu.CompilerParams(
            dimension_semantics=("arbitrary", "arbitrary"),
            vmem_limit_bytes=48 * 1024 * 1024),
        name="dispatch",
    )(x1, scale2, shift2, pos_flat, xs0)


def _experts_kernel(te_ref, tv_ref, xs_ref, wg_ref, wu_ref, wd_ref, y_ref):
    i = pl.program_id(0)

    @pl.when(tv_ref[i] > 0)
    def _():
        xb = xs_ref[...].astype(BF16)
        gate = _dot(xb, wg_ref[0])
        up = _dot(xb, wu_ref[0])
        hid = (_silu(gate) * up).astype(BF16)
        y_ref[...] = _dot(hid, wd_ref[0])

    @pl.when(tv_ref[i] == 0)
    def _():
        y_ref[...] = jnp.zeros_like(y_ref)


def _experts(xs, tile_expert, tile_valid, wg, wu, wd):
    n_rows = xs.shape[0]
    tm = MOE_TM
    grid_spec = pltpu.PrefetchScalarGridSpec(
        num_scalar_prefetch=2,
        grid=(n_rows // tm,),
        in_specs=[
            pl.BlockSpec((tm, D_MODEL), lambda i, te, tv: (i, 0)),
            pl.BlockSpec((1, D_MODEL, MOE_D_FF), lambda i, te, tv: (te[i], 0, 0)),
            pl.BlockSpec((1, D_MODEL, MOE_D_FF), lambda i, te, tv: (te[i], 0, 0)),
            pl.BlockSpec((1, MOE_D_FF, D_MODEL), lambda i, te, tv: (te[i], 0, 0)),
        ],
        out_specs=pl.BlockSpec((tm, D_MODEL), lambda i, te, tv: (i, 0)),
    )
    return pl.pallas_call(
        _experts_kernel,
        grid_spec=grid_spec,
        out_shape=jax.ShapeDtypeStruct((n_rows, D_MODEL), F32),
        compiler_params=pltpu.CompilerParams(
            dimension_semantics=("arbitrary",),
            vmem_limit_bytes=48 * 1024 * 1024),
        name="experts",
    )(tile_expert, tile_valid, xs, wg, wu, wd)


def _combine_kernel(x1_ref, route_ref, gate2_ref, lng_ref, lnb_ref, pos_ref, y_ref, o_ref,
                    ybuf, idx_ref, idx_sem, row_sem):
    tm = x1_ref.shape[1]
    n = 2 * tm
    step = pl.program_id(0) * pl.num_programs(1) + pl.program_id(1)
    nsteps = pl.num_programs(0) * pl.num_programs(1)
    slot = step % 2

    def start_gather(s):
        def start_row(i):
            for k in range(2):
                p = idx_ref[s * n + 2 * i + k]
                pltpu.make_async_copy(y_ref.at[pl.ds(p, 1)], ybuf.at[s, k, pl.ds(i, 1)], row_sem.at[s]).start()

        _for_each_row(tm, start_row)

    @pl.when(step == 0)
    def _():
        first = _idx_copy(pos_ref, idx_ref, idx_sem, 0, 0, n)
        first.start()
        first.wait()
        start_gather(0)

        @pl.when(nsteps > 1)
        def _():
            _idx_copy(pos_ref, idx_ref, idx_sem, 1, 1, n).start()

    @pl.when(step + 1 < nsteps)
    def _():
        _idx_copy(pos_ref, idx_ref, idx_sem, step + 1, 1 - slot, n).wait()
        start_gather(1 - slot)

    @pl.when(step + 2 < nsteps)
    def _():
        _idx_copy(pos_ref, idx_ref, idx_sem, step + 2, slot, n).start()

    for k in range(2):
        pltpu.make_async_copy(y_ref.at[pl.ds(0, tm)], ybuf.at[slot, k], row_sem.at[slot]).wait()

    route = route_ref[0]
    lane = lax.broadcasted_iota(jnp.int32, route.shape, 1)
    w0 = jnp.sum(jnp.where(lane == ROUTE_W0, route, 0.0), axis=1, keepdims=True)
    w1 = jnp.sum(jnp.where(lane == ROUTE_W1, route, 0.0), axis=1, keepdims=True)
    y = w0 * ybuf[slot, 0] + w1 * ybuf[slot, 1]
    o_ref[0] = _layer_norm(DEEPNORM_ALPHA * x1_ref[0] + gate2_ref[0] * y, lng_ref[...], lnb_ref[...])


def _combine_ln(x1, route, gate2, ln_g, ln_b, pos_flat, y_rows):
    bsz, seq, _ = x1.shape
    tm = min(MOE_TM, seq)
    const = lambda shape: pl.BlockSpec(shape, lambda b, i: (0,) * len(shape))
    return pl.pallas_call(
        _combine_kernel,
        grid=(bsz, seq // tm),
        in_specs=[
            pl.BlockSpec((1, tm, D_MODEL), lambda b, i: (b, i, 0)),
            pl.BlockSpec((1, tm, LANES), lambda b, i: (b, i, 0)),
            pl.BlockSpec((1, 1, D_MODEL), lambda b, i: (b, 0, 0)),
            const((1, D_MODEL)),
            const((1, D_MODEL)),
            pl.BlockSpec(memory_space=pl.ANY),
            pl.BlockSpec(memory_space=pl.ANY),
        ],
        out_specs=pl.BlockSpec((1, tm, D_MODEL), lambda b, i: (b, i, 0)),
        out_shape=jax.ShapeDtypeStruct((bsz, seq, D_MODEL), F32),
        scratch_shapes=[
            pltpu.VMEM((2, 2, tm, D_MODEL), F32),
            pltpu.SMEM((4 * tm,), jnp.int32),
            pltpu.SemaphoreType.DMA((2,)),
            pltpu.SemaphoreType.DMA((2,)),
        ],
        compiler_params=pltpu.CompilerParams(
            dimension_semantics=("arbitrary", "arbitrary"),
            vmem_limit_bytes=48 * 1024 * 1024),
        name="combine_ln",
    )(x1, route, gate2, ln_g.reshape(1, D_MODEL), ln_b.reshape(1, D_MODEL), pos_flat, y_rows)


def _route_plan(route, counts, n_tokens):
    tm = MOE_TM
    flat = route.reshape(n_tokens, LANES)
    expert = flat[:, ROUTE_E0:ROUTE_E1 + 1].astype(jnp.int32)
    rank = flat[:, ROUTE_R0:ROUTE_R1 + 1].astype(jnp.int32)
    cnt = counts[0, :MOE_EXPERTS].astype(jnp.int32)
    padded = ((cnt + tm - 1) // tm) * tm
    seg_end = jnp.cumsum(padded)
    seg_start = seg_end - padded
    pos = (seg_start[expert] + rank).reshape(-1)
    n_rows = 2 * n_tokens + MOE_EXPERTS * tm
    tile_row = jnp.arange(n_rows // tm, dtype=jnp.int32) * tm
    tile_expert = jnp.minimum(jnp.sum(tile_row[:, None] >= seg_end[None, :], axis=1), MOE_EXPERTS - 1)
    tile_valid = (tile_row < seg_end[-1]).astype(jnp.int32)
    return pos, tile_expert.astype(jnp.int32), tile_valid, n_rows


def _layer(x, c, w_ada, b_ada, w_in, conv_w, conv_b, dt_bias, a_log, d_skip, ssd_norm_w, w_sb_out, w_ssd_out,
           w_out, ln1_g, ln1_b, w_group, b_group, w_router, b_router, w_gate_e, w_up_e, w_down_e, ln2_g, ln2_b):
    bsz, seq, _ = x.shape
    mod = _ada_mod(c, w_ada, b_ada)
    shift1, scale1, gate1, shift2, scale2, gate2 = [
        m.reshape(bsz, 1, D_MODEL) for m in jnp.split(mod, 6, axis=-1)]

    o_q, o_k, o_v, o_z = 0, SB_WIDTH, 2 * SB_WIDTH, 3 * SB_WIDTH
    o_xbc = o_z + SSD_D_INNER
    o_dt = o_xbc + SSD_CONV_DIM
    o_gsb = o_dt + SSD_HEADS
    o_gssd = o_gsb + D_MODEL
    cols = lambda start, width: w_in[:, start:start + width]
    w_cat = jnp.concatenate([
        cols(o_xbc, SSD_CONV_DIM), cols(o_z, SSD_D_INNER), cols(o_gsb, D_MODEL), cols(o_gssd, D_MODEL),
        cols(o_q, SB_WIDTH), cols(o_k, SB_WIDTH), cols(o_v, SB_WIDTH)], axis=1).astype(BF16)
    w_dt = jnp.pad(cols(o_dt, SSD_HEADS), ((0, 0), (0, LANES - SSD_HEADS)))
    wdt_hi = w_dt.astype(BF16)
    wdt_lo = (w_dt - wdt_hi.astype(F32)).astype(BF16)

    proj, dt = _in_proj(x, scale1, shift1, w_cat, wdt_hi, wdt_lo)
    y_sb = _sb_attention(proj)
    y_ssd = _ssd(proj, dt, conv_w, conv_b, dt_bias, a_log, d_skip, ssd_norm_w)

    w_r = jnp.pad(jnp.concatenate([w_group, w_router], axis=1),
                  ((0, 0), (0, LANES - MOE_GROUPS - MOE_EXPERTS)))
    wr_hi = w_r.astype(BF16)
    wr_lo = (w_r - wr_hi.astype(F32)).astype(BF16)
    b_r = jnp.pad(jnp.concatenate([b_group, b_router]), (0, LANES - MOE_GROUPS - MOE_EXPERTS)).reshape(1, LANES)
    x1, route, counts = _mix_out(y_sb, y_ssd, proj, x, gate1, scale2, shift2,
                                 w_sb_out.astype(BF16), w_ssd_out.astype(BF16), w_out.astype(BF16),
                                 ln1_g, ln1_b, wr_hi, wr_lo, b_r)

    pos, tile_expert, tile_valid, n_rows = _route_plan(route, counts, bsz * seq)
    xs = _dispatch(x1, scale2, shift2, pos, n_rows)
    y_rows = _experts(xs, tile_expert, tile_valid,
                      w_gate_e.astype(BF16), w_up_e.astype(BF16), w_down_e.astype(BF16))
    return _combine_ln(x1, route, gate2, ln2_g, ln2_b, pos, y_rows)


def kernel(x, c, w_ada, b_ada, w_in, conv_w, conv_b, dt_bias, a_log, d_skip, ssd_norm_w, w_sb_out, w_ssd_out,
           w_out, ln1_g, ln1_b, w_group, b_group, w_router, b_router, w_gate_e, w_up_e, w_down_e, ln2_g, ln2_b):
    assert w_ada.shape[0] == DEPTH
    for l in range(DEPTH):
        x = _layer(x, c, w_ada[l], b_ada[l], w_in[l], conv_w[l], conv_b[l], dt_bias[l], a_log[l], d_skip[l],
                   ssd_norm_w[l], w_sb_out[l], w_ssd_out[l], w_out[l], ln1_g[l], ln1_b[l], w_group[l],
                   b_group[l], w_router[l], b_router[l], w_gate_e[l], w_up_e[l], w_down_e[l], ln2_g[l],
                   ln2_b[l])
    return x
```

```python
import functools

import jax
import jax.numpy as jnp
import numpy as np
from jax import lax
from jax.experimental import pallas as pl
from jax.experimental.pallas import tpu as pltpu

F32 = jnp.float32
BF16 = jnp.bfloat16

D_MODEL = 1024
SB_HEADS = 16
SB_HEAD_DIM = 64
SB_WIDTH = SB_HEADS * SB_HEAD_DIM
SSD_D_INNER = 2 * D_MODEL
SSD_HEAD_DIM = 64
SSD_HEADS = SSD_D_INNER // SSD_HEAD_DIM
SSD_GROUPS = 8
SSD_HEADS_PER_GROUP = SSD_HEADS // SSD_GROUPS
SSD_D_STATE = 128
SSD_CONV = 4
SSD_CHUNK = 128
SSD_GROUP_WIDTH = SSD_D_INNER // SSD_GROUPS
SSD_CONV_DIM = SSD_D_INNER + 2 * SSD_GROUPS * SSD_D_STATE
MOE_GROUPS = 4
MOE_EXPERTS_PER_GROUP = 8
MOE_EXPERTS = MOE_GROUPS * MOE_EXPERTS_PER_GROUP
MOE_D_FF = 512
DEPTH = 1
DEEPNORM_ALPHA = (2 * DEPTH) ** 0.25
LN_EPS = 1e-5
RMS_EPS = 1e-5
LOG2E = 1.4426950408889634

LANES = 128
SUBLANES = 8

COL_XBC = 0
COL_Z = COL_XBC + SSD_CONV_DIM
COL_G = COL_Z + SSD_D_INNER
COL_Q = COL_G + 2 * D_MODEL
COL_K = COL_Q + SB_WIDTH
COL_V = COL_K + SB_WIDTH
PROJ_COLS = COL_V + SB_WIDTH

IN_TM = 1024
IN_TN = 2816
SB_TQ = 256
SB_HEADS_PER_STEP = 4
MIX_TM = 512
MOE_TM = 512
SB_LOG_WEIGHT_FLOOR = -110.0

ROUTE_E0, ROUTE_E1, ROUTE_R0, ROUTE_R1, ROUTE_W0, ROUTE_W1 = range(6)


def _split3(a):
    a1 = a.astype(BF16)
    r1 = a - a1.astype(F32)
    a2 = r1.astype(BF16)
    a3 = (r1 - a2.astype(F32)).astype(BF16)
    return a1, a2, a3


def _dot(a, b):
    return jnp.dot(a, b, preferred_element_type=F32)


def _dot_nt(a, b):
    return lax.dot_general(a, b, (((1,), (1,)), ((), ())), preferred_element_type=F32)


def _sigmoid(x):
    return 0.5 * jnp.tanh(0.5 * x) + 0.5


def _silu(x):
    half = 0.5 * x
    return half * jnp.tanh(half) + half


def _softplus(x):
    return jnp.maximum(x, 0.0) + jnp.log(1.0 + jnp.exp(-jnp.abs(x)))


def _ada_kernel(c_ref, w_ref, b_ref, o_ref):
    c = c_ref[...]
    o_ref[...] = _dot(_silu(c), w_ref[...]) + b_ref[...]


def _ada_mod(c, w_ada, b_ada):
    bsz = c.shape[0]
    n = w_ada.shape[1]
    tn = D_MODEL
    return pl.pallas_call(
        _ada_kernel,
        grid=(n // tn,),
        in_specs=[
            pl.BlockSpec((bsz, D_MODEL), lambda j: (0, 0)),
            pl.BlockSpec((D_MODEL, tn), lambda j: (0, j)),
            pl.BlockSpec((1, tn), lambda j: (0, j)),
        ],
        out_specs=pl.BlockSpec((bsz, tn), lambda j: (0, j)),
        out_shape=jax.ShapeDtypeStruct((bsz, n), F32),
        name="ada_mod",
    )(c, w_ada, b_ada.reshape(1, n))


def _in_proj_kernel(x_ref, sc_ref, sh_ref, w_ref, wdt_hi_ref, wdt_lo_ref, o_ref, dt_ref, h_ref):
    j = pl.program_id(2)

    @pl.when(j == 0)
    def _():
        h = x_ref[0] * (1.0 + sc_ref[0]) + sh_ref[0]
        h1, h2, _ = _split3(h)
        h_ref[...] = h1
        dt_ref[0] = _dot(h1, wdt_hi_ref[...]) + _dot(h2, wdt_hi_ref[...]) + _dot(h1, wdt_lo_ref[...])

    o_ref[0] = _dot(h_ref[...], w_ref[...]).astype(BF16)


def _in_proj(x, scale1, shift1, w_cat, wdt_hi, wdt_lo):
    bsz, seq, _ = x.shape
    tm, tn = min(IN_TM, seq), IN_TN
    return pl.pallas_call(
        _in_proj_kernel,
        grid=(bsz, seq // tm, PROJ_COLS // tn),
        in_specs=[
            pl.BlockSpec((1, tm, D_MODEL), lambda b, i, j: (b, i, 0)),
            pl.BlockSpec((1, 1, D_MODEL), lambda b, i, j: (b, 0, 0)),
            pl.BlockSpec((1, 1, D_MODEL), lambda b, i, j: (b, 0, 0)),
            pl.BlockSpec((D_MODEL, tn), lambda b, i, j: (0, j)),
            pl.BlockSpec((D_MODEL, LANES), lambda b, i, j: (0, 0)),
            pl.BlockSpec((D_MODEL, LANES), lambda b, i, j: (0, 0)),
        ],
        out_specs=[
            pl.BlockSpec((1, tm, tn), lambda b, i, j: (b, i, j)),
            pl.BlockSpec((1, tm, LANES), lambda b, i, j: (b, i, 0)),
        ],
        out_shape=[
            jax.ShapeDtypeStruct((bsz, seq, PROJ_COLS), BF16),
            jax.ShapeDtypeStruct((bsz, seq, LANES), F32),
        ],
        scratch_shapes=[pltpu.VMEM((tm, D_MODEL), BF16)],
        compiler_params=pltpu.CompilerParams(
            dimension_semantics=("parallel", "parallel", "arbitrary"),
            vmem_limit_bytes=48 * 1024 * 1024),
        name="in_proj",
    )(x, scale1, shift1, w_cat, wdt_hi, wdt_lo)


def _sb_kernel(q_ref, k_ref, v_ref, o_ref, acc_ref, r_ref, *, tq, scale):
    qi = pl.program_id(2)
    tk = tq
    fold_scale = float(np.log2(scale)).is_integer()
    q = q_ref[0]
    if fold_scale:
        q = q * scale
    width = q_ref.shape[2]
    heads = width // SB_HEAD_DIM
    rows = heads * tq
    lane = lax.broadcasted_iota(jnp.int32, (tq, width), 1)
    q_stack = jnp.concatenate(
        [jnp.where((lane >= h * SB_HEAD_DIM) & (lane < (h + 1) * SB_HEAD_DIM), q, jnp.zeros_like(q))
         for h in range(heads)], axis=0)
    row_in_tile = jnp.concatenate([lax.broadcasted_iota(jnp.int32, (tq, tk), 0)] * heads, axis=0)
    strictly_past = lax.broadcasted_iota(jnp.int32, (rows, tk), 1) < row_in_tile
    later_key = (lax.broadcasted_iota(jnp.int32, (tk, tk), 0)
                 > lax.broadcasted_iota(jnp.int32, (tk, tk), 1)).astype(BF16)

    def tile(kt, carry, diag):
        start = pl.multiple_of(kt * tk, tk)
        k_blk = k_ref[0, pl.ds(start, tk), :]
        v_blk = v_ref[0, pl.ds(start, tk), :]
        s = _dot_nt(q_stack, k_blk)
        if not fold_scale:
            s = s * scale
        log_sig = jnp.minimum(s, 0.0) - jnp.log(1.0 + jnp.exp2(jnp.abs(s) * (-LOG2E)))
        log_not = log_sig - s
        if diag:
            log_not = jnp.where(strictly_past, log_not, 0.0)
        between = _dot(log_not.astype(BF16), later_key)
        logw = log_sig + between
        if carry is not None:
            logw = logw + carry
        w = jnp.exp(logw)
        if diag:
            w = jnp.where(strictly_past, w, 0.0)
        return _dot(w.astype(BF16), v_blk), jnp.sum(log_not, axis=1, keepdims=True)

    def newest_tiles(with_previous):
        acc, r = tile(qi, None, True)
        if with_previous:
            acc_p, r_p = tile(qi - 1, r, False)
            acc = acc + acc_p
            r = r + r_p
        acc_ref[...] = acc
        r_ref[...] = jnp.broadcast_to(r, (rows, LANES))
        return jnp.max(r)

    @pl.when(qi == 0)
    def _():
        newest_tiles(False)

    @pl.when(qi > 0)
    def _():
        top = newest_tiles(True)

        def cond(carry):
            kt, go = carry
            return jnp.logical_and(kt >= 0, go > 0)

        def body(carry):
            kt, _ = carry
            r = r_ref[...]
            acc, r_t = tile(kt, jnp.concatenate([r] * (tk // LANES), axis=1), False)
            acc_ref[...] += acc
            r_new = r + r_t
            r_ref[...] = r_new
            return kt - 1, (jnp.max(r_new) > SB_LOG_WEIGHT_FLOOR).astype(jnp.int32)

        lax.while_loop(cond, body, (qi - 2, (top > SB_LOG_WEIGHT_FLOOR).astype(jnp.int32)))

    out = acc_ref[0:tq, :]
    for h in range(1, heads):
        out = jnp.where(lane >= h * SB_HEAD_DIM, acc_ref[h * tq:(h + 1) * tq, :], out)
    o_ref[0] = out.astype(BF16)


def _sb_attention(proj):
    bsz, seq, _ = proj.shape
    tq = min(SB_TQ, seq)
    width = SB_HEADS_PER_STEP * SB_HEAD_DIM
    kern = functools.partial(_sb_kernel, tq=tq, scale=SB_HEAD_DIM ** -0.5)
    return pl.pallas_call(
        kern,
        grid=(bsz, SB_WIDTH // width, seq // tq),
        in_specs=[
            pl.BlockSpec((1, tq, width), lambda b, p, i: (b, i, COL_Q // width + p)),
            pl.BlockSpec((1, seq, width), lambda b, p, i: (b, 0, COL_K // width + p)),
            pl.BlockSpec((1, seq, width), lambda b, p, i: (b, 0, COL_V // width + p)),
        ],
        out_specs=pl.BlockSpec((1, tq, width), lambda b, p, i: (b, i, p)),
        out_shape=jax.ShapeDtypeStruct((bsz, seq, SB_WIDTH), BF16),
        scratch_shapes=[pltpu.VMEM((SB_HEADS_PER_STEP * tq, width), F32),
                        pltpu.VMEM((SB_HEADS_PER_STEP * tq, LANES), F32)],
        compiler_params=pltpu.CompilerParams(
            dimension_semantics=("parallel", "parallel", "arbitrary")),
        name="sb_attn",
    )(proj, proj, proj)


def _ssd_kernel(xbc_ref, z_ref, dt_ref, cw_ref, cb_ref, dtb_ref, alog_ref, dskip_ref, nw_ref, e2_ref,
                o_ref, ubuf, state_ref):
    c = pl.program_id(1)
    L = SSD_CHUNK
    GW = SSD_GROUP_WIDTH
    NS = SSD_D_STATE
    halo = SUBLANES

    @pl.when(c == 0)
    def _():
        state_ref[...] = jnp.zeros_like(state_ref)
        ubuf[0:halo, :] = jnp.zeros((halo, SSD_CONV_DIM), F32)

    ubuf[halo:halo + L, :] = xbc_ref[0].astype(F32)

    def conv(c0, width):
        window = ubuf[:, c0:c0 + width]
        acc = cb_ref[:, c0:c0 + width] + cw_ref[SSD_CONV - 1:SSD_CONV, c0:c0 + width] * window[halo:halo + L]
        for kk in range(SSD_CONV - 1):
            delayed = pltpu.roll(window, SSD_CONV - 1 - kk, axis=0)[halo:halo + L]
            acc = acc + cw_ref[kk:kk + 1, c0:c0 + width] * delayed
        return _silu(acc)

    lane = lax.broadcasted_iota(jnp.int32, (1, LANES), 1)
    head_lane = lane < SSD_HEADS
    dtv = _softplus(dt_ref[0] + dtb_ref[...])
    a = jnp.where(head_lane, -jnp.exp(alog_ref[...]), 0.0)
    adt = a * dtv
    li = lax.broadcasted_iota(jnp.int32, (L, L), 0)
    si = lax.broadcasted_iota(jnp.int32, (L, L), 1)
    causal = si <= li
    tri = causal.astype(BF16)
    a1, a2, a3 = _split3(adt)
    cs = _dot(tri, a1) + _dot(tri, a2) + _dot(tri, a3)
    cs_t = cs.T
    cs_end = cs[L - 1:L, :]
    decay_to_end = jnp.exp(cs_end - cs)
    decay_from_start = jnp.exp(cs)

    def expand(v, g):
        hi = v.astype(BF16)
        lo = (v - hi.astype(F32)).astype(BF16)
        return _dot(jnp.concatenate([hi, lo], axis=1), e2_ref[:, g * GW:(g + 1) * GW])

    lane_gw = lax.broadcasted_iota(jnp.int32, (L, GW), 1)

    for g in range(SSD_GROUPS):
        xs = conv(g * GW, GW)
        bmat = conv(SSD_D_INNER + g * NS, NS).astype(BF16)
        cmat = conv(SSD_D_INNER + SSD_GROUPS * NS + g * NS, NS).astype(BF16)
        cb = _dot_nt(cmat, bmat)
        dt_e = expand(dtv, g)
        xdt = xs * dt_e
        xdt_b = xdt.astype(BF16)
        xdte_b = (xdt * expand(decay_to_end, g)).astype(BF16)
        from_start_e = expand(decay_from_start, g)
        masked_cb, head_x = [], []
        for r in range(SSD_HEADS_PER_GROUP):
            h = g * SSD_HEADS_PER_GROUP + r
            seg = jnp.broadcast_to(cs[:, h:h + 1], (L, L)) - cs_t[h:h + 1, :]
            decay = jnp.exp(jnp.where(causal, seg, -jnp.inf))
            masked_cb.append((cb * decay).astype(BF16))
            in_head = (lane_gw >= r * SSD_HEAD_DIM) & (lane_gw < (r + 1) * SSD_HEAD_DIM)
            head_x.append(jnp.where(in_head, xdt_b, jnp.zeros_like(xdt_b)))
        y = _dot(jnp.concatenate(masked_cb, axis=1), jnp.concatenate(head_x, axis=0))
        state = state_ref[g]
        y = y + _dot(cmat, state.astype(BF16)) * from_start_e
        new_state = lax.dot_general(bmat, xdte_b, (((0,), (0,)), ((), ())), preferred_element_type=F32)
        state_ref[g] = state * from_start_e[L - 1:L, :] + new_state
        y = y + dskip_ref[:, g * GW:(g + 1) * GW] * xs
        zg = z_ref[0, :, g * GW:(g + 1) * GW].astype(F32)
        yg = y * _silu(zg)
        ms = jnp.mean(yg * yg, axis=1, keepdims=True)
        o_ref[0, :, g * GW:(g + 1) * GW] = (
            yg * lax.rsqrt(ms + RMS_EPS) * nw_ref[:, g * GW:(g + 1) * GW]).astype(BF16)

    ubuf[0:halo, :] = ubuf[L:L + halo, :]


def _ssd(proj, dt, conv_w, conv_b, dt_bias, a_log, d_skip, norm_w):
    bsz, seq, _ = proj.shape
    L = SSD_CHUNK
    pad = LANES - SSD_HEADS
    dtb = jnp.pad(dt_bias, (0, pad)).reshape(1, LANES)
    alog = jnp.pad(a_log, (0, pad)).reshape(1, LANES)
    dskip_e = jnp.repeat(d_skip, SSD_HEAD_DIM).reshape(1, SSD_D_INNER)
    head_of_col = np.arange(SSD_D_INNER) // SSD_HEAD_DIM
    e = (np.arange(LANES)[:, None] == head_of_col[None, :]).astype(np.float32)
    e2 = jnp.asarray(np.concatenate([e, e], axis=0), dtype=BF16)
    const = lambda shape: pl.BlockSpec(shape, lambda b, c: (0,) * len(shape))
    return pl.pallas_call(
        _ssd_kernel,
        grid=(bsz, seq // L),
        in_specs=[
            pl.BlockSpec((1, L, SSD_CONV_DIM), lambda b, c: (b, c, COL_XBC // SSD_CONV_DIM)),
            pl.BlockSpec((1, L, SSD_D_INNER), lambda b, c: (b, c, COL_Z // SSD_D_INNER)),
            pl.BlockSpec((1, L, LANES), lambda b, c: (b, c, 0)),
            const((SSD_CONV, SSD_CONV_DIM)),
            const((1, SSD_CONV_DIM)),
            const((1, LANES)),
            const((1, LANES)),
            const((1, SSD_D_INNER)),
            const((1, SSD_D_INNER)),
            const((2 * LANES, SSD_D_INNER)),
        ],
        out_specs=pl.BlockSpec((1, L, SSD_D_INNER), lambda b, c: (b, c, 0)),
        out_shape=jax.ShapeDtypeStruct((bsz, seq, SSD_D_INNER), BF16),
        scratch_shapes=[
            pltpu.VMEM((L + SUBLANES, SSD_CONV_DIM), F32),
            pltpu.VMEM((SSD_GROUPS, SSD_D_STATE, SSD_GROUP_WIDTH), F32),
        ],
        compiler_params=pltpu.CompilerParams(
            dimension_semantics=("parallel", "arbitrary"),
            vmem_limit_bytes=48 * 1024 * 1024),
        name="ssd",
    )(proj, proj, dt, conv_w, conv_b.reshape(1, SSD_CONV_DIM), dtb, alog, dskip_e,
      norm_w.reshape(1, SSD_D_INNER), e2)


def _layer_norm(r, g, b):
    mu = jnp.mean(r, axis=1, keepdims=True)
    rc = r - mu
    var = jnp.mean(rc * rc, axis=1, keepdims=True)
    return rc * lax.rsqrt(var + LN_EPS) * g + b


def _mix_kernel(ysb_ref, yssd_ref, g_ref, x_ref, gate1_ref, wsb_ref, wssd_ref, wout_ref, lng_ref, lnb_ref,
                sc2_ref, sh2_ref, wr_hi_ref, wr_lo_ref, br_ref, tri_ref,
                x1_ref, route_ref, cnt_out_ref, cnt_ref):
    first = jnp.logical_and(pl.program_id(0) == 0, pl.program_id(1) == 0)

    @pl.when(first)
    def _():
        cnt_ref[...] = jnp.zeros_like(cnt_ref)

    tm = x_ref.shape[1]
    a_sb = _dot(ysb_ref[0], wsb_ref[...])
    a_ssd = _dot(yssd_ref[0], wssd_ref[...])
    gates = g_ref[0].astype(F32)
    merged = _sigmoid(gates[:, :D_MODEL]) * a_sb + _sigmoid(gates[:, D_MODEL:]) * a_ssd
    y = _dot(merged.astype(BF16), wout_ref[...])
    x1 = _layer_norm(DEEPNORM_ALPHA * x_ref[0] + gate1_ref[0] * y, lng_ref[...], lnb_ref[...])
    x1_ref[0] = x1

    h2 = x1 * (1.0 + sc2_ref[0]) + sh2_ref[0]
    h_1, h_2, _ = _split3(h2)
    logits = (_dot(h_1, wr_hi_ref[...]) + _dot(h_2, wr_hi_ref[...]) + _dot(h_1, wr_lo_ref[...])
              + br_ref[...])
    lane = lax.broadcasted_iota(jnp.int32, (tm, LANES), 1).astype(F32)
    neg = -jnp.inf
    big = float(LANES)
    gl = jnp.where(lane < MOE_GROUPS, logits, neg)
    gmax = jnp.max(gl, axis=1, keepdims=True)
    g_prob = 1.0 / jnp.sum(jnp.exp(gl - gmax), axis=1, keepdims=True)
    g_idx = jnp.min(jnp.where(gl == gmax, lane, big), axis=1, keepdims=True)
    lo = MOE_GROUPS + MOE_EXPERTS_PER_GROUP * g_idx
    l1 = jnp.where((lane >= lo) & (lane < lo + MOE_EXPERTS_PER_GROUP), logits, neg)
    m1 = jnp.max(l1, axis=1, keepdims=True)
    i1 = jnp.min(jnp.where(l1 == m1, lane, big), axis=1, keepdims=True)
    l2 = jnp.where(lane == i1, neg, l1)
    m2 = jnp.max(l2, axis=1, keepdims=True)
    i2 = jnp.min(jnp.where(l2 == m2, lane, big), axis=1, keepdims=True)
    e21 = jnp.exp(m2 - m1)
    w0 = g_prob / (1.0 + e21)
    w1 = g_prob * e21 / (1.0 + e21)
    e0 = i1 - MOE_GROUPS
    e1 = i2 - MOE_GROUPS

    sel0 = lane == e0
    sel1 = lane == e1
    onehot = jnp.where(sel0 | sel1, 1.0, 0.0)
    before = _dot(tri_ref[...], onehot.astype(BF16)) + cnt_ref[...]
    rank0 = jnp.sum(jnp.where(sel0, before, 0.0), axis=1, keepdims=True)
    rank1 = jnp.sum(jnp.where(sel1, before, 0.0), axis=1, keepdims=True)
    cnt_new = cnt_ref[...] + jnp.sum(onehot, axis=0, keepdims=True)
    cnt_ref[...] = cnt_new
    cnt_out_ref[...] = jnp.broadcast_to(cnt_new, cnt_out_ref.shape)

    route = jnp.zeros((tm, LANES), F32)
    for idx, val in ((ROUTE_E0, e0), (ROUTE_E1, e1), (ROUTE_R0, rank0), (ROUTE_R1, rank1),
                     (ROUTE_W0, w0), (ROUTE_W1, w1)):
        route = jnp.where(lane == float(idx), val, route)
    route_ref[0] = route


def _mix_out(y_sb, y_ssd, proj, x, gate1, scale2, shift2, w_sb, w_ssd, w_out, ln_g, ln_b, wr_hi, wr_lo, br):
    bsz, seq, _ = x.shape
    tm = min(MIX_TM, seq)
    tri = jnp.asarray(np.tril(np.ones((tm, tm), np.float32), -1), dtype=BF16)
    const = lambda shape: pl.BlockSpec(shape, lambda b, i: (0,) * len(shape))
    per_batch = pl.BlockSpec((1, 1, D_MODEL), lambda b, i: (b, 0, 0))
    return pl.pallas_call(
        _mix_kernel,
        grid=(bsz, seq // tm),
        in_specs=[
            pl.BlockSpec((1, tm, SB_WIDTH), lambda b, i: (b, i, 0)),
            pl.BlockSpec((1, tm, SSD_D_INNER), lambda b, i: (b, i, 0)),
            pl.BlockSpec((1, tm, 2 * D_MODEL), lambda b, i: (b, i, COL_G // (2 * D_MODEL))),
            pl.BlockSpec((1, tm, D_MODEL), lambda b, i: (b, i, 0)),
            per_batch,
            const((SB_WIDTH, D_MODEL)),
            const((SSD_D_INNER, D_MODEL)),
            const((D_MODEL, D_MODEL)),
            const((1, D_MODEL)),
            const((1, D_MODEL)),
            per_batch,
            per_batch,
            const((D_MODEL, LANES)),
            const((D_MODEL, LANES)),
            const((1, LANES)),
            const((tm, tm)),
        ],
        out_specs=[
            pl.BlockSpec((1, tm, D_MODEL), lambda b, i: (b, i, 0)),
            pl.BlockSpec((1, tm, LANES), lambda b, i: (b, i, 0)),
            pl.BlockSpec((SUBLANES, LANES), lambda b, i: (0, 0)),
        ],
        out_shape=[
            jax.ShapeDtypeStruct((bsz, seq, D_MODEL), F32),
            jax.ShapeDtypeStruct((bsz, seq, LANES), F32),
            jax.ShapeDtypeStruct((SUBLANES, LANES), F32),
        ],
        scratch_shapes=[pltpu.VMEM((1, LANES), F32)],
        compiler_params=pltpu.CompilerParams(
            dimension_semantics=("arbitrary", "arbitrary"),
            vmem_limit_bytes=48 * 1024 * 1024),
        name="mix_out",
    )(y_sb, y_ssd, proj, x, gate1, w_sb, w_ssd, w_out, ln_g.reshape(1, D_MODEL), ln_b.reshape(1, D_MODEL),
      scale2, shift2, wr_hi, wr_lo, br, tri)


def _idx_copy(pos_ref, idx_ref, sem_ref, step, slot, n):
    return pltpu.make_async_copy(pos_ref.at[pl.ds(pl.multiple_of(step * n, n), n)],
                                 idx_ref.at[pl.ds(pl.multiple_of(slot * n, n), n)], sem_ref.at[slot])


def _for_each_row(tm, fn):
    def body(io, carry):
        base = pl.multiple_of(io * SUBLANES, SUBLANES)
        for j in range(SUBLANES):
            fn(base + j)
        return carry

    lax.fori_loop(0, tm // SUBLANES, body, 0)


def _dispatch_kernel(x1_ref, sc2_ref, sh2_ref, pos_ref, xs_in_ref, xs_ref, hbuf, idx_ref, idx_sem, row_sem):
    del xs_in_ref
    tm = x1_ref.shape[1]
    n = 2 * tm
    step = pl.program_id(0) * pl.num_programs(1) + pl.program_id(1)
    nsteps = pl.num_programs(0) * pl.num_programs(1)
    slot = step % 2

    def wait_rows(s):
        for _ in range(2):
            pltpu.make_async_copy(hbuf.at[s], xs_ref.at[pl.ds(0, tm)], row_sem.at[s]).wait()

    @pl.when(step == 0)
    def _():
        _idx_copy(pos_ref, idx_ref, idx_sem, 0, 0, n).start()

    _idx_copy(pos_ref, idx_ref, idx_sem, step, slot, n).wait()

    @pl.when(step + 1 < nsteps)
    def _():
        _idx_copy(pos_ref, idx_ref, idx_sem, step + 1, 1 - slot, n).start()

    hbuf[slot] = x1_ref[0] * (1.0 + sc2_ref[0]) + sh2_ref[0]

    def start_row(i):
        for k in range(2):
            p = idx_ref[slot * n + 2 * i + k]
            pltpu.make_async_copy(hbuf.at[slot, pl.ds(i, 1)], xs_ref.at[pl.ds(p, 1)], row_sem.at[slot]).start()

    _for_each_row(tm, start_row)

    @pl.when(step > 0)
    def _():
        wait_rows(1 - slot)

    @pl.when(step + 1 == nsteps)
    def _():
        wait_rows(slot)


def _dispatch(x1, scale2, shift2, pos_flat, n_rows):
    bsz, seq, _ = x1.shape
    tm = min(MOE_TM, seq)
    xs0 = jnp.zeros((n_rows, D_MODEL), F32)
    per_batch = pl.BlockSpec((1, 1, D_MODEL), lambda b, i: (b, 0, 0))
    return pl.pallas_call(
        _dispatch_kernel,
        grid=(bsz, seq // tm),
        in_specs=[
            pl.BlockSpec((1, tm, D_MODEL), lambda b, i: (b, i, 0)),
            per_batch,
            per_batch,
            pl.BlockSpec(memory_space=pl.ANY),
            pl.BlockSpec(memory_space=pl.ANY),
        ],
        out_specs=pl.BlockSpec(memory_space=pl.ANY),
        out_shape=jax.ShapeDtypeStruct((n_rows, D_MODEL), F32),
        scratch_shapes=[
            pltpu.VMEM((2, tm, D_MODEL), F32),
            pltpu.SMEM((4 * tm,), jnp.int32),
            pltpu.SemaphoreType.DMA((2,)),
            pltpu.SemaphoreType.DMA((2,)),
        ],
        input_output_aliases={4: 0},
        compiler_params=pltpu.CompilerParams(
            dimension_semantics=("arbitrary", "arbitrary"),
            vmem_limit_bytes=48 * 1024 * 1024),
        name="dispatch",
    )(x1, scale2, shift2, pos_flat, xs0)


def _experts_kernel(te_ref, tv_ref, xs_ref, wg_ref, wu_ref, wd_ref, y_ref):
    i = pl.program_id(0)

    @pl.when(tv_ref[i] > 0)
    def _():
        xb = xs_ref[...].astype(BF16)
        gate = _dot(xb, wg_ref[0])
        up = _dot(xb, wu_ref[0])
        hid = (_silu(gate) * up).astype(BF16)
        y_ref[...] = _dot(hid, wd_ref[0])

    @pl.when(tv_ref[i] == 0)
    def _():
        y_ref[...] = jnp.zeros_like(y_ref)


def _experts(xs, tile_expert, tile_valid, wg, wu, wd):
    n_rows = xs.shape[0]
    tm = MOE_TM
    grid_spec = pltpu.PrefetchScalarGridSpec(
        num_scalar_prefetch=2,
        grid=(n_rows // tm,),
        in_specs=[
            pl.BlockSpec((tm, D_MODEL), lambda i, te, tv: (i, 0)),
            pl.BlockSpec((1, D_MODEL, MOE_D_FF), lambda i, te, tv: (te[i], 0, 0)),
            pl.BlockSpec((1, D_MODEL, MOE_D_FF), lambda i, te, tv: (te[i], 0, 0)),
            pl.BlockSpec((1, MOE_D_FF, D_MODEL), lambda i, te, tv: (te[i], 0, 0)),
        ],
        out_specs=pl.BlockSpec((tm, D_MODEL), lambda i, te, tv: (i, 0)),
    )
    return pl.pallas_call(
        _experts_kernel,
        grid_spec=grid_spec,
        out_shape=jax.ShapeDtypeStruct((n_rows, D_MODEL), F32),
        compiler_params=pltpu.CompilerParams(
            dimension_semantics=("arbitrary",),
            vmem_limit_bytes=48 * 1024 * 1024),
        name="experts",
    )(tile_expert, tile_valid, xs, wg, wu, wd)


def _combine_kernel(x1_ref, route_ref, gate2_ref, lng_ref, lnb_ref, pos_ref, y_ref, o_ref,
                    ybuf, idx_ref, idx_sem, row_sem):
    tm = x1_ref.shape[1]
    n = 2 * tm
    step = pl.program_id(0) * pl.num_programs(1) + pl.program_id(1)
    nsteps = pl.num_programs(0) * pl.num_programs(1)
    slot = step % 2

    def start_gather(s):
        def start_row(i):
            for k in range(2):
                p = idx_ref[s * n + 2 * i + k]
                pltpu.make_async_copy(y_ref.at[pl.ds(p, 1)], ybuf.at[s, k, pl.ds(i, 1)], row_sem.at[s]).start()

        _for_each_row(tm, start_row)

    @pl.when(step == 0)
    def _():
        first = _idx_copy(pos_ref, idx_ref, idx_sem, 0, 0, n)
        first.start()
        first.wait()
        start_gather(0)

        @pl.when(nsteps > 1)
        def _():
            _idx_copy(pos_ref, idx_ref, idx_sem, 1, 1, n).start()

    @pl.when(step + 1 < nsteps)
    def _():
        _idx_copy(pos_ref, idx_ref, idx_sem, step + 1, 1 - slot, n).wait()
        start_gather(1 - slot)

    @pl.when(step + 2 < nsteps)
    def _():
        _idx_copy(pos_ref, idx_ref, idx_sem, step + 2, slot, n).start()

    for k in range(2):
        pltpu.make_async_copy(y_ref.at[pl.ds(0, tm)], ybuf.at[slot, k], row_sem.at[slot]).wait()

    route = route_ref[0]
    lane = lax.broadcasted_iota(jnp.int32, route.shape, 1)
    w0 = jnp.sum(jnp.where(lane == ROUTE_W0, route, 0.0), axis=1, keepdims=True)
    w1 = jnp.sum(jnp.where(lane == ROUTE_W1, route, 0.0), axis=1, keepdims=True)
    y = w0 * ybuf[slot, 0] + w1 * ybuf[slot, 1]
    o_ref[0] = _layer_norm(DEEPNORM_ALPHA * x1_ref[0] + gate2_ref[0] * y, lng_ref[...], lnb_ref[...])


def _combine_ln(x1, route, gate2, ln_g, ln_b, pos_flat, y_rows):
    bsz, seq, _ = x1.shape
    tm = min(MOE_TM, seq)
    const = lambda shape: pl.BlockSpec(shape, lambda b, i: (0,) * len(shape))
    return pl.pallas_call(
        _combine_kernel,
        grid=(bsz, seq // tm),
        in_specs=[
            pl.BlockSpec((1, tm, D_MODEL), lambda b, i: (b, i, 0)),
            pl.BlockSpec((1, tm, LANES), lambda b, i: (b, i, 0)),
            pl.BlockSpec((1, 1, D_MODEL), lambda b, i: (b, 0, 0)),
            const((1, D_MODEL)),
            const((1, D_MODEL)),
            pl.BlockSpec(memory_space=pl.ANY),
            pl.BlockSpec(memory_space=pl.ANY),
        ],
        out_specs=pl.BlockSpec((1, tm, D_MODEL), lambda b, i: (b, i, 0)),
        out_shape=jax.ShapeDtypeStruct((bsz, seq, D_MODEL), F32),
        scratch_shapes=[
            pltpu.VMEM((2, 2, tm, D_MODEL), F32),
            pltpu.SMEM((4 * tm,), jnp.int32),
            pltpu.SemaphoreType.DMA((2,)),
            pltpu.SemaphoreType.DMA((2,)),
        ],
        compiler_params=pltpu.CompilerParams(
            dimension_semantics=("arbitrary", "arbitrary"),
            vmem_limit_bytes=48 * 1024 * 1024),
        name="combine_ln",
    )(x1, route, gate2, ln_g.reshape(1, D_MODEL), ln_b.reshape(1, D_MODEL), pos_flat, y_rows)


def _route_plan(route, counts, n_tokens):
    tm = MOE_TM
    flat = route.reshape(n_tokens, LANES)
    expert = flat[:, ROUTE_E0:ROUTE_E1 + 1].astype(jnp.int32)
    rank = flat[:, ROUTE_R0:ROUTE_R1 + 1].astype(jnp.int32)
    cnt = counts[0, :MOE_EXPERTS].astype(jnp.int32)
    padded = ((cnt + tm - 1) // tm) * tm
    seg_end = jnp.cumsum(padded)
    seg_start = seg_end - padded
    pos = (seg_start[expert] + rank).reshape(-1)
    n_rows = 2 * n_tokens + MOE_EXPERTS * tm
    tile_row = jnp.arange(n_rows // tm, dtype=jnp.int32) * tm
    tile_expert = jnp.minimum(jnp.sum(tile_row[:, None] >= seg_end[None, :], axis=1), MOE_EXPERTS - 1)
    tile_valid = (tile_row < seg_end[-1]).astype(jnp.int32)
    return pos, tile_expert.astype(jnp.int32), tile_valid, n_rows


def _layer(x, c, w_ada, b_ada, w_in, conv_w, conv_b, dt_bias, a_log, d_skip, ssd_norm_w, w_sb_out, w_ssd_out,
           w_out, ln1_g, ln1_b, w_group, b_group, w_router, b_router, w_gate_e, w_up_e, w_down_e, ln2_g, ln2_b):
    bsz, seq, _ = x.shape
    mod = _ada_mod(c, w_ada, b_ada)
    shift1, scale1, gate1, shift2, scale2, gate2 = [
        m.reshape(bsz, 1, D_MODEL) for m in jnp.split(mod, 6, axis=-1)]

    o_q, o_k, o_v, o_z = 0, SB_WIDTH, 2 * SB_WIDTH, 3 * SB_WIDTH
    o_xbc = o_z + SSD_D_INNER
    o_dt = o_xbc + SSD_CONV_DIM
    o_gsb = o_dt + SSD_HEADS
    o_gssd = o_gsb + D_MODEL
    cols = lambda start, width: w_in[:, start:start + width]
    w_cat = jnp.concatenate([
        cols(o_xbc, SSD_CONV_DIM), cols(o_z, SSD_D_INNER), cols(o_gsb, D_MODEL), cols(o_gssd, D_MODEL),
        cols(o_q, SB_WIDTH), cols(o_k, SB_WIDTH), cols(o_v, SB_WIDTH)], axis=1).astype(BF16)
    w_dt = jnp.pad(cols(o_dt, SSD_HEADS), ((0, 0), (0, LANES - SSD_HEADS)))
    wdt_hi = w_dt.astype(BF16)
    wdt_lo = (w_dt - wdt_hi.astype(F32)).astype(BF16)

    proj, dt = _in_proj(x, scale1, shift1, w_cat, wdt_hi, wdt_lo)
    y_sb = _sb_attention(proj)
    y_ssd = _ssd(proj, dt, conv_w, conv_b, dt_bias, a_log, d_skip, ssd_norm_w)

    w_r = jnp.pad(jnp.concatenate([w_group, w_router], axis=1),
                  ((0, 0), (0, LANES - MOE_GROUPS - MOE_EXPERTS)))
    wr_hi = w_r.astype(BF16)
    wr_lo = (w_r - wr_hi.astype(F32)).astype(BF16)
    b_r = jnp.pad(jnp.concatenate([b_group, b_router]), (0, LANES - MOE_GROUPS - MOE_EXPERTS)).reshape(1, LANES)
    x1, route, counts = _mix_out(y_sb, y_ssd, proj, x, gate1, scale2, shift2,
                                 w_sb_out.astype(BF16), w_ssd_out.astype(BF16), w_out.astype(BF16),
                                 ln1_g, ln1_b, wr_hi, wr_lo, b_r)

    pos, tile_expert, tile_valid, n_rows = _route_plan(route, counts, bsz * seq)
    xs = _dispatch(x1, scale2, shift2, pos, n_rows)
    y_rows = _experts(xs, tile_expert, tile_valid,
                      w_gate_e.astype(BF16), w_up_e.astype(BF16), w_down_e.astype(BF16))
    return _combine_ln(x1, route, gate2, ln2_g, ln2_b, pos, y_rows)


def kernel(x, c, w_ada, b_ada, w_in, conv_w, conv_b, dt_bias, a_log, d_skip, ssd_norm_w, w_sb_out, w_ssd_out,
           w_out, ln1_g, ln1_b, w_group, b_group, w_router, b_router, w_gate_e, w_up_e, w_down_e, ln2_g, ln2_b):
    assert w_ada.shape[0] == DEPTH
    for l in range(DEPTH):
        x = _layer(x, c, w_ada[l], b_ada[l], w_in[l], conv_w[l], conv_b[l], dt_bias[l], a_log[l], d_skip[l],
                   ssd_norm_w[l], w_sb_out[l], w_ssd_out[l], w_out[l], ln1_g[l], ln1_b[l], w_group[l],
                   b_group[l], w_router[l], b_router[l], w_gate_e[l], w_up_e[l], w_down_e[l], ln2_g[l],
                   ln2_b[l])
    return x
```
